```python
import jax, jax.numpy as jnp
from jax import lax
import numpy as np

D_MODEL = 1024
BATCH = 8
SEQ = 2048
DEPTH = 4
DEC_BATCH = 128
DEC_SEQ = 8
PAST_LEN = 16384
PAGE_SIZE = 128

N_MIXERS = 2
N_POOL_LAYERS = (DEPTH + 1) // 2
N_SSD_LAYERS = DEPTH // 2
EXPAND = 2
D_INNER = EXPAND * D_MODEL
POOL_WINDOWS = (2, 4, 8, 16)
N_POOL_GROUPS = len(POOL_WINDOWS)
POOL_GROUP = D_INNER // N_POOL_GROUPS
POOL_BUF = max(POOL_WINDOWS) - 1
HEAD_DIM = 64
N_HEADS = D_INNER // HEAD_DIM
D_STATE = 128
N_GROUPS = 4
HEADS_PER_GROUP = N_HEADS // N_GROUPS
CONV_K = 4
CONV_DIM = D_INNER + 2 * N_GROUPS * D_STATE
SSD_IN = D_INNER + CONV_DIM + N_HEADS
CHUNK = 128
EPS = 1e-6

kernel_name = 'pool_ssd_hybrid_step'


def rmsnorm(x, w):
    xf = x.astype(jnp.float32)
    y = xf * lax.rsqrt(jnp.mean(xf * xf, axis=-1, keepdims=True) + EPS)
    return (y * w.astype(jnp.float32)).astype(x.dtype)


def pool_mixer(h, buf, start, w_in, w_mix, scale, w_out):
    b, l, _ = h.shape
    uz = h @ w_in
    u, z = uz[..., :D_INNER], uz[..., D_INNER:]
    u_ext = jnp.concatenate([buf.astype(u.dtype), u], axis=1)
    csum = jnp.cumsum(u_ext.astype(jnp.float32), axis=1)
    csum = jnp.concatenate([jnp.zeros((b, 1, D_INNER), jnp.float32), csum], axis=1)
    hi = csum[:, POOL_BUF + 1:]
    pos = start + jnp.arange(l)
    uf = u.astype(jnp.float32)
    outs = []
    for g, w in enumerate(POOL_WINDOWS):
        sl = slice(g * POOL_GROUP, (g + 1) * POOL_GROUP)
        lo = csum[:, POOL_BUF + 1 - w:POOL_BUF + 1 - w + l, sl]
        cnt = jnp.minimum(pos + 1, w).astype(jnp.float32)[None, :, None]
        outs.append((hi[..., sl] - lo) / cnt - uf[..., sl])
    p = jnp.stack(outs, axis=2)
    mixed = jnp.einsum('blgc,gcd->blgd', p, w_mix.astype(jnp.float32)).reshape(b, l, D_INNER)
    y = mixed * scale.astype(jnp.float32) * jax.nn.silu(z.astype(jnp.float32))
    return y.astype(h.dtype) @ w_out, u_ext[:, -POOL_BUF:]


def causal_conv(xbc, buf, w, bias):
    l = xbc.shape[1]
    ext = jnp.concatenate([buf.astype(xbc.dtype), xbc], axis=1)
    out = ext[:, 0:l] * w[0]
    for k in range(1, CONV_K):
        out = out + ext[:, k:k + l] * w[k]
    return jax.nn.silu(out + bias), ext[:, -(CONV_K - 1):]


def ssd_scan(x, dt, A, B, C, h0):
    b, l = x.shape[:2]
    q = min(CHUNK, l)
    nc = l // q
    x = x.reshape(b, nc, q, N_GROUPS, HEADS_PER_GROUP, HEAD_DIM)
    dt = dt.reshape(b, nc, q, N_GROUPS, HEADS_PER_GROUP)
    B = B.reshape(b, nc, q, N_GROUPS, D_STATE)
    C = C.reshape(b, nc, q, N_GROUPS, D_STATE)
    a = dt * A.reshape(N_GROUPS, HEADS_PER_GROUP)
    a_cum_t = jnp.moveaxis(jnp.cumsum(a, axis=2), 2, -1)
    dt_t = jnp.moveaxis(dt, 2, -1)
    causal = jnp.tril(jnp.ones((q, q), bool))
    seg = a_cum_t[..., :, None] - a_cum_t[..., None, :]
    L = jnp.exp(jnp.where(causal, seg, -jnp.inf))
    CB = jnp.einsum('bcign,bcjgn->bcgij', C, B)
    W = CB[:, :, :, None] * L * dt_t[..., None, :]
    y_diag = jnp.einsum('bcgeij,bcjgep->bcigep', W, x)
    decay_to_end = jnp.exp(a_cum_t[..., -1:] - a_cum_t) * dt_t
    chunk_states = jnp.einsum('bcjgn,bcgej,bcjgep->bcgepn', B, decay_to_end, x)
    chunk_decay = jnp.exp(a_cum_t[..., -1])

    def step(hs, inp):
        s, d = inp
        return d[..., None, None] * hs + s, hs

    h0g = h0.reshape(b, N_GROUPS, HEADS_PER_GROUP, HEAD_DIM, D_STATE)
    h_final, h_prev = lax.scan(step, h0g, (jnp.moveaxis(chunk_states, 1, 0), jnp.moveaxis(chunk_decay, 1, 0)))
    h_prev = jnp.moveaxis(h_prev, 0, 1)
    y_off = jnp.einsum('bcign,bcgepn,bcgei->bcigep', C, h_prev, jnp.exp(a_cum_t))
    y = (y_diag + y_off).reshape(b, l, N_HEADS, HEAD_DIM)
    return y, h_final.reshape(b, N_HEADS, HEAD_DIM, D_STATE)


def ssd_mixer(h, conv_buf, ssm_state, w_in, conv_w, conv_b, dt_bias, A_log, D_skip, norm_w, w_out):
    b, l, _ = h.shape
    proj = h @ w_in
    z = proj[..., :D_INNER]
    xbc = proj[..., D_INNER:D_INNER + CONV_DIM]
    dt_raw = proj[..., D_INNER + CONV_DIM:]
    xbc, new_conv = causal_conv(xbc, conv_buf, conv_w, conv_b)
    xbc = xbc.astype(jnp.float32)
    xs = xbc[..., :D_INNER].reshape(b, l, N_HEADS, HEAD_DIM)
    Bm = xbc[..., D_INNER:D_INNER + N_GROUPS * D_STATE].reshape(b, l, N_GROUPS, D_STATE)
    Cm = xbc[..., D_INNER + N_GROUPS * D_STATE:].reshape(b, l, N_GROUPS, D_STATE)
    dt = jax.nn.softplus(dt_raw.astype(jnp.float32) + dt_bias.astype(jnp.float32))
    A = -jnp.exp(A_log.astype(jnp.float32))
    y, new_ssm = ssd_scan(xs, dt, A, Bm, Cm, ssm_state.astype(jnp.float32))
    y = y + D_skip.astype(jnp.float32)[:, None] * xs
    y = y.reshape(b, l, D_INNER) * jax.nn.silu(z.astype(jnp.float32))
    y = rmsnorm(y, norm_w)
    return y.astype(h.dtype) @ w_out, new_conv, new_ssm.astype(ssm_state.dtype)


def trunk(x, pool_buf, conv_buf, ssm_buf, start, norm_w, pool_in_w, pool_mix_w, pool_scale, pool_out_w,
          ssd_in_w, ssd_conv_w, ssd_conv_b, ssd_dt_bias, ssd_A_log, ssd_D, ssd_norm_w, ssd_out_w, final_norm_w):
    new_pool, new_conv, new_ssm = [], [], []
    for i in range(DEPTH):
        h = rmsnorm(x, norm_w[i])
        j = i // N_MIXERS
        if i % N_MIXERS == 0:
            out, nb = pool_mixer(h, pool_buf[j], start, pool_in_w[j], pool_mix_w[j], pool_scale[j], pool_out_w[j])
            new_pool.append(nb)
        else:
            out, nc, ns = ssd_mixer(h, conv_buf[j], ssm_buf[j], ssd_in_w[j], ssd_conv_w[j], ssd_conv_b[j],
                                    ssd_dt_bias[j], ssd_A_log[j], ssd_D[j], ssd_norm_w[j], ssd_out_w[j])
            new_conv.append(nc)
            new_ssm.append(ns)
        x = x + out
    return rmsnorm(x, final_norm_w), jnp.stack(new_pool), jnp.stack(new_conv), jnp.stack(new_ssm)


def setup_inputs(seed: int = 0) -> dict:
    key = jax.random.key(seed)
    ks = jax.random.split(key, 20)
    f32 = jnp.float32
    nrm = lambda k, s, sc: jax.random.normal(k, s, f32) * sc
    dt0 = jnp.exp(jax.random.uniform(ks[14], (N_SSD_LAYERS, N_HEADS), f32, np.log(1e-3), np.log(1e-1)))
    return {
        'x_prompt': nrm(ks[0], (BATCH, SEQ, D_MODEL), 1.0),
        'x_sample': nrm(ks[1], (DEC_BATCH, DEC_SEQ, D_MODEL), 1.0),
        'state_pool': nrm(ks[2], (N_POOL_LAYERS, DEC_BATCH, POOL_BUF, D_INNER), 1.0),
        'state_conv': nrm(ks[3], (N_SSD_LAYERS, DEC_BATCH, CONV_K - 1, CONV_DIM), 1.0),
        'state_ssm': nrm(ks[4], (N_SSD_LAYERS, DEC_BATCH, N_HEADS, HEAD_DIM, D_STATE), 0.1),
        'norm_w': 1.0 + nrm(ks[5], (DEPTH, D_MODEL), 0.02),
        'pool_in_w': nrm(ks[6], (N_POOL_LAYERS, D_MODEL, 2 * D_INNER), D_MODEL ** -0.5),
        'pool_mix_w': nrm(ks[7], (N_POOL_LAYERS, N_POOL_GROUPS, POOL_GROUP, POOL_GROUP), POOL_GROUP ** -0.5),
        'pool_scale': 1.0 + nrm(ks[8], (N_POOL_LAYERS, D_INNER), 0.1),
        'pool_out_w': nrm(ks[9], (N_POOL_LAYERS, D_INNER, D_MODEL), D_INNER ** -0.5),
        'ssd_in_w': nrm(ks[10], (N_SSD_LAYERS, D_MODEL, SSD_IN), D_MODEL ** -0.5),
        'ssd_conv_w': nrm(ks[11], (N_SSD_LAYERS, CONV_K, CONV_DIM), CONV_K ** -0.5),
        'ssd_conv_b': nrm(ks[12], (N_SSD_LAYERS, CONV_DIM), 0.01),
        'ssd_dt_bias': dt0 + jnp.log(-jnp.expm1(-dt0)),
        'ssd_A_log': jnp.log(jax.random.uniform(ks[13], (N_SSD_LAYERS, N_HEADS), f32, 1.0, 16.0)),
        'ssd_D': 1.0 + nrm(ks[15], (N_SSD_LAYERS, N_HEADS), 0.1),
        'ssd_norm_w': 1.0 + nrm(ks[16], (N_SSD_LAYERS, D_INNER), 0.02),
        'ssd_out_w': nrm(ks[17], (N_SSD_LAYERS, D_INNER, D_MODEL), D_INNER ** -0.5),
        'final_norm_w': 1.0 + nrm(ks[18], (D_MODEL,), 0.02),
    }


def reference(x_prompt, x_sample, state_pool, state_conv, state_ssm, norm_w, pool_in_w, pool_mix_w, pool_scale,
              pool_out_w, ssd_in_w, ssd_conv_w, ssd_conv_b, ssd_dt_bias, ssd_A_log, ssd_D, ssd_norm_w, ssd_out_w,
              final_norm_w):
    b = x_prompt.shape[0]
    dt = x_prompt.dtype
    zero_pool = jnp.zeros((N_POOL_LAYERS, b, POOL_BUF, D_INNER), dt)
    zero_conv = jnp.zeros((N_SSD_LAYERS, b, CONV_K - 1, CONV_DIM), dt)
    zero_ssm = jnp.zeros((N_SSD_LAYERS, b, N_HEADS, HEAD_DIM, D_STATE), state_ssm.dtype)
    y_prompt, pool_p, conv_p, ssm_p = trunk(
        x_prompt, zero_pool, zero_conv, zero_ssm, 0, norm_w, pool_in_w, pool_mix_w, pool_scale, pool_out_w,
        ssd_in_w, ssd_conv_w, ssd_conv_b, ssd_dt_bias, ssd_A_log, ssd_D, ssd_norm_w, ssd_out_w, final_norm_w)
    y_sample, pool_s, conv_s, ssm_s = trunk(
        x_sample, state_pool, state_conv, state_ssm, PAST_LEN, norm_w, pool_in_w, pool_mix_w, pool_scale, pool_out_w,
        ssd_in_w, ssd_conv_w, ssd_conv_b, ssd_dt_bias, ssd_A_log, ssd_D, ssd_norm_w, ssd_out_w, final_norm_w)
    return (y_prompt, y_sample, pool_p, pool_s, conv_p, conv_s, ssm_p, ssm_s)
```

```python
import functools

import jax
import jax.numpy as jnp
from jax import lax
from jax.experimental import pallas as pl
from jax.experimental.pallas import tpu as pltpu

D_MODEL = 1024
DEPTH = 4
PAST_LEN = 16384
D_INNER = 2 * D_MODEL
POOL_WINDOWS = (2, 4, 8, 16)
POOL_GROUP = D_INNER // len(POOL_WINDOWS)
POOL_BUF = max(POOL_WINDOWS) - 1
HEAD_DIM = 64
N_HEADS = D_INNER // HEAD_DIM
D_STATE = 128
N_GROUPS = 4
HEADS_PER_GROUP = N_HEADS // N_GROUPS
GROUP_W = HEADS_PER_GROUP * HEAD_DIM
CONV_K = 4
CONV_DIM = D_INNER + 2 * N_GROUPS * D_STATE
CHUNK = 128
EPS = 1e-6

F32 = jnp.float32
BF16 = jnp.bfloat16

SUBLANES = 8
VMEM_LIMIT_BYTES = 56 * 1024 * 1024

POOL_TL = 512
SSD_TL = 256
POOL_BT = 32
SSD_BT = 16
SSD_SB = 2


def _const_spec(shape):
    zeros = (0,) * len(shape)
    return pl.BlockSpec(shape, lambda *_: zeros, pipeline_mode=pl.Buffered(1))


def _rms(x, w):
    return x * lax.rsqrt(jnp.mean(x * x, axis=-1, keepdims=True) + EPS) * w


def _silu(x):
    return x * jax.nn.sigmoid(x)


def _softplus(x):
    return jnp.maximum(x, 0.0) + jnp.log1p(jnp.exp(-jnp.abs(x)))


def _split3(v):
    v1 = v.astype(BF16).astype(F32)
    r1 = v - v1
    v2 = r1.astype(BF16).astype(F32)
    v3 = (r1 - v2).astype(BF16).astype(F32)
    return v1, v2, v3


def _sum3(f, v):
    p1, p2, p3 = (f(p.astype(BF16)) for p in _split3(v))
    return p1 + p2 + p3


def _dot01(m01, v):
    return _sum3(lambda p: jnp.dot(m01, p, preferred_element_type=F32), v)


def _dot01_tn(v, m01):
    dn = (((0,), (0,)), ((), ()))
    return _sum3(lambda p: lax.dot_general(p, m01, dn, preferred_element_type=F32), v)


def _dot01_r(v, m01):
    return _sum3(lambda p: jnp.dot(p, m01, preferred_element_type=F32), v)


def _expand_heads(v, r3):
    parts = jnp.concatenate(_split3(v), axis=1).astype(BF16)
    return jnp.dot(parts, r3, preferred_element_type=F32)


def _pool_tail(p_of_group, z_of_group, wmix_ref, scale_ref, wout_ref):
    acc = None
    for g in range(len(POOL_WINDOWS)):
        cols = slice(g * POOL_GROUP, (g + 1) * POOL_GROUP)
        mixed = jnp.dot(p_of_group(g).astype(BF16), wmix_ref[g], preferred_element_type=F32)
        y = mixed * scale_ref[:, cols] * _silu(z_of_group(g))
        part = jnp.dot(y.astype(BF16), wout_ref[cols, :], preferred_element_type=F32)
        acc = part if acc is None else acc + part
    return acc


def _pool_prompt_kernel(x_ref, nw_ref, win_ref, wmix_ref, scale_ref, wout_ref, xo_ref, st_ref, ext_ref, *, tl, nt):
    t = pl.program_id(1)
    hist = 2 * SUBLANES
    x = x_ref[0]
    h = _rms(x, nw_ref[...]).astype(BF16)

    @pl.when(t == 0)
    def _():
        ext_ref[0:hist, :] = jnp.zeros((hist, D_INNER), F32)

    ext_ref[hist:hist + tl, :] = jnp.dot(h, win_ref[:, :D_INNER], preferred_element_type=F32)
    pos = t * tl + lax.broadcasted_iota(jnp.int32, (tl, 1), 0)

    def p_of_group(g):
        w = POOL_WINDOWS[g]
        cols = slice(g * POOL_GROUP, (g + 1) * POOL_GROUP)
        u = ext_ref[hist:hist + tl, cols]
        s = u
        for k in range(1, w):
            s = s + ext_ref[hist - k:hist - k + tl, cols]
        cnt = jnp.minimum(pos + 1, w).astype(F32)
        return s * (1.0 / cnt) - u

    def z_of_group(g):
        c0 = D_INNER + g * POOL_GROUP
        return jnp.dot(h, win_ref[:, c0:c0 + POOL_GROUP], preferred_element_type=F32)

    xo_ref[0] = x + _pool_tail(p_of_group, z_of_group, wmix_ref, scale_ref, wout_ref)

    @pl.when(t == nt - 1)
    def _():
        st_ref[0] = ext_ref[hist + tl - POOL_BUF:hist + tl, :]

    ext_ref[0:hist, :] = ext_ref[tl:tl + hist, :]


def _pool_prompt(x, nw, win, wmix, scale, wout):
    b, l, _ = x.shape
    tl = POOL_TL
    nt = l // tl
    kern = functools.partial(_pool_prompt_kernel, tl=tl, nt=nt)
    return pl.pallas_call(
        kern,
        grid=(b, nt),
        in_specs=[
            pl.BlockSpec((1, tl, D_MODEL), lambda i, t: (i, t, 0)),
            _const_spec((1, D_MODEL)),
            _const_spec((D_MODEL, 2 * D_INNER)),
            _const_spec((len(POOL_WINDOWS), POOL_GROUP, POOL_GROUP)),
            _const_spec((1, D_INNER)),
            _const_spec((D_INNER, D_MODEL)),
        ],
        out_specs=[
            pl.BlockSpec((1, tl, D_MODEL), lambda i, t: (i, t, 0)),
            pl.BlockSpec((1, POOL_BUF, D_INNER), lambda i, t: (i, 0, 0)),
        ],
        out_shape=[
            jax.ShapeDtypeStruct((b, l, D_MODEL), F32),
            jax.ShapeDtypeStruct((b, POOL_BUF, D_INNER), F32),
        ],
        scratch_shapes=[pltpu.VMEM((tl + 2 * SUBLANES, D_INNER), F32)],
        compiler_params=pltpu.CompilerParams(
            dimension_semantics=("arbitrary", "arbitrary"), vmem_limit_bytes=VMEM_LIMIT_BYTES),
        name="pool_prompt",
    )(x, nw, win, wmix, scale, wout)


def _pool_sample_kernel(x_ref, buf_ref, nw_ref, win_ref, wmix_ref, scale_ref, wout_ref, xo_ref, st_ref, ext_ref,
                        *, bt, seq, start):
    hist = 2 * SUBLANES
    x = x_ref[...]
    h = _rms(x, nw_ref[...]).astype(BF16)
    u = jnp.dot(h, win_ref[:, :D_INNER], preferred_element_type=F32)
    ext_ref[:, SUBLANES:hist, :] = jnp.zeros((bt, SUBLANES, D_INNER), F32)
    ext_ref[:, 0:POOL_BUF, :] = buf_ref[0]
    ext_ref[:, hist:hist + seq, :] = u.reshape(bt, seq, D_INNER)
    trow = lax.broadcasted_iota(jnp.int32, (1, seq, 1), 1)

    def p_of_group(g):
        w = POOL_WINDOWS[g]
        cols = slice(g * POOL_GROUP, (g + 1) * POOL_GROUP)
        cur = ext_ref[:, hist:hist + seq, cols]
        s = cur
        for k in range(1, w):
            lower = ext_ref[:, hist - 1 - k:hist - 1 - k + seq, cols]
            if k >= seq:
                term = lower
            else:
                term = jnp.where(trow >= k, ext_ref[:, hist - k:hist - k + seq, cols], lower)
            s = s + term
        cnt = jnp.minimum(start + trow + 1, w).astype(F32)
        return (s * (1.0 / cnt) - cur).reshape(bt * seq, POOL_GROUP)

    def z_of_group(g):
        c0 = D_INNER + g * POOL_GROUP
        return jnp.dot(h, win_ref[:, c0:c0 + POOL_GROUP], preferred_element_type=F32)

    xo_ref[...] = x + _pool_tail(p_of_group, z_of_group, wmix_ref, scale_ref, wout_ref)
    keep = POOL_BUF - seq
    st_ref[0, :, 0:keep, :] = ext_ref[:, seq:POOL_BUF, :]
    st_ref[0, :, keep:POOL_BUF, :] = ext_ref[:, hist:hist + seq, :]


def _pool_sample(x, state, layer, nw, win, wmix, scale, wout, *, seq):
    rows = x.shape[0]
    nb = rows // seq
    bt = POOL_BT
    kern = functools.partial(_pool_sample_kernel, bt=bt, seq=seq, start=PAST_LEN)
    return pl.pallas_call(
        kern,
        grid=(nb // bt,),
        in_specs=[
            pl.BlockSpec((bt * seq, D_MODEL), lambda i: (i, 0)),
            pl.BlockSpec((1, bt, POOL_BUF, D_INNER), lambda i: (layer, i, 0, 0)),
            _const_spec((1, D_MODEL)),
            _const_spec((D_MODEL, 2 * D_INNER)),
            _const_spec((len(POOL_WINDOWS), POOL_GROUP, POOL_GROUP)),
            _const_spec((1, D_INNER)),
            _const_spec((D_INNER, D_MODEL)),
        ],
        out_specs=[
            pl.BlockSpec((bt * seq, D_MODEL), lambda i: (i, 0)),
            pl.BlockSpec((1, bt, POOL_BUF, D_INNER), lambda i: (0, i, 0, 0)),
        ],
        out_shape=[
            jax.ShapeDtypeStruct((rows, D_MODEL), F32),
            jax.ShapeDtypeStruct((1, nb, POOL_BUF, D_INNER), F32),
        ],
        scratch_shapes=[pltpu.VMEM((bt, 2 * SUBLANES + seq, D_INNER), F32)],
        compiler_params=pltpu.CompilerParams(
            dimension_semantics=("arbitrary",), vmem_limit_bytes=VMEM_LIMIT_BYTES),
        name="pool_sample",
    )(x, state, nw, win, wmix, scale, wout)


def _ssd_block_masks(q):
    i = jnp.arange(CHUNK)[:, None]
    j = jnp.arange(CHUNK)[None, :]
    same = (i // q) == (j // q)
    tril = (same & (j <= i)).astype(BF16)
    ones = same.astype(BF16)
    eye = (i == j).astype(BF16)
    return tril, tril.T, ones, eye


def _head_expander():
    h = jnp.arange(N_HEADS)[:, None]
    c = jnp.arange(D_INNER)[None, :] // HEAD_DIM
    r = (h == c).astype(BF16)
    return jnp.concatenate([r, r, r], axis=0)


def _ssd_dense_front(x, nw_ref, wxbc_ref, wdt_ref, dtb_ref, alog_ref):
    h = _rms(x, nw_ref[...]).astype(BF16)
    xbc = jnp.dot(h, wxbc_ref[...], preferred_element_type=F32)
    dt = _softplus(jnp.dot(h, wdt_ref[...], preferred_element_type=F32) + dtb_ref[...])
    a = dt * (-jnp.exp(alog_ref[...]))
    return h, xbc, dt, a


def _ssd_block_diag(r0, a_c, dt_c, act_ref, ybuf_ref, tril, triu, ones, eye, q):
    acum = _dot01(tril, a_c)
    acum_t = _dot01_tn(a_c, triu)
    alast = _dot01(ones, a_c)
    dt_t = _dot01_tn(dt_c, eye)
    ii = lax.broadcasted_iota(jnp.int32, (CHUNK, CHUNK), 0)
    jj = lax.broadcasted_iota(jnp.int32, (CHUNK, CHUNK), 1)
    causal = jj <= ii
    if q != CHUNK:
        causal = causal & ((ii // q) == (jj // q))
    rows = slice(r0, r0 + CHUNK)
    for g in range(N_GROUPS):
        b0 = D_INNER + g * D_STATE
        c0 = D_INNER + N_GROUPS * D_STATE + g * D_STATE
        bg = act_ref[rows, b0:b0 + D_STATE].astype(BF16)
        cg = act_ref[rows, c0:c0 + D_STATE].astype(BF16)
        cb = lax.dot_general(cg, bg, (((1,), (1,)), ((), ())), preferred_element_type=F32)
        for e in range(HEADS_PER_GROUP):
            hh = g * HEADS_PER_GROUP + e
            seg = acum[:, hh:hh + 1] - acum_t[hh:hh + 1, :]
            decay = jnp.exp(jnp.where(causal, seg, -jnp.inf))
            w = cb * decay * dt_t[hh:hh + 1, :]
            hcols = slice(hh * HEAD_DIM, (hh + 1) * HEAD_DIM)
            xh = act_ref[rows, hcols].astype(BF16)
            ybuf_ref[rows, hcols] = jnp.dot(w.astype(BF16), xh, preferred_element_type=F32)
    return acum, alast


def _ssd_epilogue(x, h, act_ref, ybuf_ref, wz_ref, dx_ref, normw_ref, wout_ref, fnw_ref, final):
    z = jnp.dot(h, wz_ref[...], preferred_element_type=F32)
    y = ybuf_ref[...] + dx_ref[...] * act_ref[:, :D_INNER]
    y = _rms(y * _silu(z), normw_ref[...])
    out = x + jnp.dot(y.astype(BF16), wout_ref[...], preferred_element_type=F32)
    return _rms(out, fnw_ref[...]) if final else out


def _causal_conv(ext, lo, n, cw_ref, cb_ref):
    out = ext(0) * cw_ref[0:1, :]
    for k in range(1, CONV_K):
        out = out + ext(k) * cw_ref[k:k + 1, :]
    return _silu(out + cb_ref[...])


def _ssd_prompt_kernel(x_ref, nw_ref, wz_ref, wxbc_ref, wdt_ref, cw_ref, cb_ref, dtb_ref, alog_ref, dx_ref,
                       normw_ref, wout_ref, fnw_ref, tril_ref, triu_ref, ones_ref, eye_ref, r3_ref,
                       xo_ref, cst_ref, ssm_ref, cext_ref, act_ref, ybuf_ref, ht_ref, *, tl, nt, final):
    t = pl.program_id(1)
    x = x_ref[0]

    @pl.when(t == 0)
    def _():
        cext_ref[0:SUBLANES, :] = jnp.zeros((SUBLANES, CONV_DIM), F32)
        ht_ref[...] = jnp.zeros_like(ht_ref)

    h, xbc, dt, a = _ssd_dense_front(x, nw_ref, wxbc_ref, wdt_ref, dtb_ref, alog_ref)
    cext_ref[SUBLANES:SUBLANES + tl, :] = xbc
    lo = SUBLANES - (CONV_K - 1)
    act_ref[...] = _causal_conv(lambda k: cext_ref[lo + k:lo + k + tl, :], lo, tl, cw_ref, cb_ref)

    tril, triu, ones, eye = tril_ref[...], triu_ref[...], ones_ref[...], eye_ref[...]
    for c in range(tl // CHUNK):
        r0 = c * CHUNK
        rows = slice(r0, r0 + CHUNK)
        a_c, dt_c = a[rows], dt[rows]
        acum, alast = _ssd_block_diag(r0, a_c, dt_c, act_ref, ybuf_ref, tril, triu, ones, eye, CHUNK)
        scales = jnp.concatenate(
            [jnp.exp(acum), jnp.exp(alast - acum) * dt_c, jnp.exp(alast[0:SUBLANES])], axis=0)
        scales_x = _expand_heads(scales, r3_ref[...])
        ea_x = scales_x[0:CHUNK]
        dte_x = scales_x[CHUNK:2 * CHUNK]
        cd_x = scales_x[2 * CHUNK:2 * CHUNK + 1]
        for g in range(N_GROUPS):
            gcols = slice(g * GROUP_W, (g + 1) * GROUP_W)
            b0 = D_INNER + g * D_STATE
            c0 = D_INNER + N_GROUPS * D_STATE + g * D_STATE
            bg = act_ref[rows, b0:b0 + D_STATE].astype(BF16)
            cg = act_ref[rows, c0:c0 + D_STATE].astype(BF16)
            hprev = ht_ref[:, gcols]
            y_off = jnp.dot(cg, hprev.astype(BF16), preferred_element_type=F32) * ea_x[:, gcols]
            ybuf_ref[rows, gcols] = ybuf_ref[rows, gcols] + y_off
            xs = (act_ref[rows, gcols] * dte_x[:, gcols]).astype(BF16)
            st = lax.dot_general(bg, xs, (((0,), (0,)), ((), ())), preferred_element_type=F32)
            ht_ref[:, gcols] = hprev * cd_x[:, gcols] + st

    xo_ref[0] = _ssd_epilogue(x, h, act_ref, ybuf_ref, wz_ref, dx_ref, normw_ref, wout_ref, fnw_ref, final)

    @pl.when(t == nt - 1)
    def _():
        cst_ref[0] = cext_ref[SUBLANES + tl - (CONV_K - 1):SUBLANES + tl, :]
        ssm_ref[0] = ht_ref[...].T

    cext_ref[0:SUBLANES, :] = cext_ref[tl:tl + SUBLANES, :]


def _ssd_weight_specs():
    return [
        _const_spec((1, D_MODEL)),
        _const_spec((D_MODEL, D_INNER)),
        _const_spec((D_MODEL, CONV_DIM)),
        _const_spec((D_MODEL, N_HEADS)),
        _const_spec((CONV_K, CONV_DIM)),
        _const_spec((1, CONV_DIM)),
        _const_spec((1, N_HEADS)),
        _const_spec((1, N_HEADS)),
        _const_spec((1, D_INNER)),
        _const_spec((1, D_INNER)),
        _const_spec((D_INNER, D_MODEL)),
        _const_spec((1, D_MODEL)),
        _const_spec((CHUNK, CHUNK)),
        _const_spec((CHUNK, CHUNK)),
        _const_spec((CHUNK, CHUNK)),
        _const_spec((CHUNK, CHUNK)),
        _const_spec((3 * N_HEADS, D_INNER)),
    ]


def _ssd_prompt(x, weights, *, final):
    b, l, _ = x.shape
    tl = SSD_TL
    nt = l // tl
    kern = functools.partial(_ssd_prompt_kernel, tl=tl, nt=nt, final=final)
    masks = _ssd_block_masks(CHUNK)
    return pl.pallas_call(
        kern,
        grid=(b, nt),
        in_specs=[pl.BlockSpec((1, tl, D_MODEL), lambda i, t: (i, t, 0))] + _ssd_weight_specs(),
        out_specs=[
            pl.BlockSpec((1, tl, D_MODEL), lambda i, t: (i, t, 0)),
            pl.BlockSpec((1, CONV_K - 1, CONV_DIM), lambda i, t: (i, 0, 0)),
            pl.BlockSpec((1, N_HEADS * HEAD_DIM, D_STATE), lambda i, t: (i, 0, 0)),
        ],
        out_shape=[
            jax.ShapeDtypeStruct((b, l, D_MODEL), F32),
            jax.ShapeDtypeStruct((b, CONV_K - 1, CONV_DIM), F32),
            jax.ShapeDtypeStruct((b, N_HEADS * HEAD_DIM, D_STATE), F32),
        ],
        scratch_shapes=[
            pltpu.VMEM((tl + SUBLANES, CONV_DIM), F32),
            pltpu.VMEM((tl, CONV_DIM), F32),
            pltpu.VMEM((tl, D_INNER), F32),
            pltpu.VMEM((D_STATE, N_HEADS * HEAD_DIM), F32),
        ],
        compiler_params=pltpu.CompilerParams(
            dimension_semantics=("arbitrary", "arbitrary"), vmem_limit_bytes=VMEM_LIMIT_BYTES),
        name="ssd_prompt",
    )(x, *weights, *masks, _head_expander())


def _ssd_sample_kernel(x_ref, cin_ref, hin_ref, nw_ref, wz_ref, wxbc_ref, wdt_ref, cw_ref, cb_ref, dtb_ref,
                       alog_ref, dx_ref, normw_ref, wout_ref, fnw_ref, tril_ref, triu_ref, ones_ref, eye_ref,
                       r3_ref, xo_ref, cst_ref, hout_ref, cext_ref, act_ref, ybuf_ref, eax_ref, xsc_ref,
                       cdt_ref, h_ref, *, bt, sb, seq, final):
    s_idx = pl.program_id(1)
    n_inner = bt // sb
    rows_total = bt * seq

    @pl.when(s_idx == 0)
    def _():
        x = x_ref[...]
        h, xbc, dt, a = _ssd_dense_front(x, nw_ref, wxbc_ref, wdt_ref, dtb_ref, alog_ref)
        h_ref[...] = h
        lo = SUBLANES - (CONV_K - 1)
        cext_ref[:, 0:SUBLANES, :] = jnp.zeros((bt, SUBLANES, CONV_DIM), F32)
        cext_ref[:, lo:SUBLANES, :] = cin_ref[0]
        cext_ref[:, SUBLANES:SUBLANES + seq, :] = xbc.reshape(bt, seq, CONV_DIM)
        conv = _causal_conv(lambda k: cext_ref[:, lo + k:lo + k + seq, :], lo, seq, cw_ref, cb_ref)
        act_ref[...] = conv.reshape(rows_total, CONV_DIM)
        cst_ref[0] = cext_ref[:, SUBLANES + seq - (CONV_K - 1):SUBLANES + seq, :]
        tril, triu, ones, eye = tril_ref[...], triu_ref[...], ones_ref[...], eye_ref[...]
        for c in range(rows_total // CHUNK):
            r0 = c * CHUNK
            rows = slice(r0, r0 + CHUNK)
            a_c, dt_c = a[rows], dt[rows]
            acum, alast = _ssd_block_diag(r0, a_c, dt_c, act_ref, ybuf_ref, tril, triu, ones, eye, seq)
            scales = jnp.concatenate([jnp.exp(acum), jnp.exp(alast - acum) * dt_c], axis=0)
            scales_x = _expand_heads(scales, r3_ref[...])
            eax_ref[rows, :] = scales_x[0:CHUNK]
            xsc_ref[rows, :] = act_ref[rows, 0:D_INNER] * scales_x[CHUNK:2 * CHUNK]
            cdt_ref[:, rows] = jnp.exp(_dot01_tn(a_c, ones))

    for si in range(sb):
        r0 = pl.multiple_of((s_idx * sb + si) * seq, seq)
        rows = pl.ds(r0, seq)
        pick = (lax.broadcasted_iota(jnp.int32, (rows_total, D_STATE), 0) == r0).astype(BF16)
        cd = _dot01_r(cdt_ref[...], pick)
        for g in range(N_GROUPS):
            gcols = slice(g * GROUP_W, (g + 1) * GROUP_W)
            b0 = D_INNER + g * D_STATE
            c0 = D_INNER + N_GROUPS * D_STATE + g * D_STATE
            bg = act_ref[rows, b0:b0 + D_STATE].astype(BF16)
            cg = act_ref[rows, c0:c0 + D_STATE].astype(BF16)
            hprev = hin_ref[0, si, g * GROUP_W:(g + 1) * GROUP_W, :]
            y_off = lax.dot_general(cg, hprev.astype(BF16), (((1,), (1,)), ((), ())), preferred_element_type=F32)
            ybuf_ref[rows, gcols] = ybuf_ref[rows, gcols] + y_off * eax_ref[rows, gcols]
            st = lax.dot_general(xsc_ref[rows, gcols].astype(BF16), bg, (((0,), (0,)), ((), ())),
                                 preferred_element_type=F32)
            for e in range(HEADS_PER_GROUP):
                hh = g * HEADS_PER_GROUP + e
                hrows = slice(hh * HEAD_DIM, (hh + 1) * HEAD_DIM)
                hout_ref[0, si, hrows, :] = (hin_ref[0, si, hrows, :] * cd[hh:hh + 1, :]
                                             + st[e * HEAD_DIM:(e + 1) * HEAD_DIM, :])

    @pl.when(s_idx == n_inner - 1)
    def _():
        xo_ref[...] = _ssd_epilogue(x_ref[...], h_ref[...], act_ref, ybuf_ref, wz_ref, dx_ref, normw_ref,
                                    wout_ref, fnw_ref, final)


def _ssd_sample(x, state_conv, state_ssm, layer, weights, *, seq, final):
    rows = x.shape[0]
    nb = rows // seq
    bt, sb = SSD_BT, SSD_SB
    n_inner = bt // sb
    kern = functools.partial(_ssd_sample_kernel, bt=bt, sb=sb, seq=seq, final=final)
    masks = _ssd_block_masks(seq)
    hp = N_HEADS * HEAD_DIM
    return pl.pallas_call(
        kern,
        grid=(nb // bt, n_inner),
        in_specs=[
            pl.BlockSpec((bt * seq, D_MODEL), lambda i, s: (i, 0)),
            pl.BlockSpec((1, bt, CONV_K - 1, CONV_DIM), lambda i, s: (layer, i, 0, 0)),
            pl.BlockSpec((1, sb, hp, D_STATE), lambda i, s: (layer, i * n_inner + s, 0, 0)),
        ] + _ssd_weight_specs(),
        out_specs=[
            pl.BlockSpec((bt * seq, D_MODEL), lambda i, s: (i, 0)),
            pl.BlockSpec((1, bt, CONV_K - 1, CONV_DIM), lambda i, s: (0, i, 0, 0)),
            pl.BlockSpec((1, sb, hp, D_STATE), lambda i, s: (0, i * n_inner + s, 0, 0)),
        ],
        out_shape=[
            jax.ShapeDtypeStruct((rows, D_MODEL), F32),
            jax.ShapeDtypeStruct((1, nb, CONV_K - 1, CONV_DIM), F32),
            jax.ShapeDtypeStruct((1, nb, hp, D_STATE), F32),
        ],
        scratch_shapes=[
            pltpu.VMEM((bt, SUBLANES + seq, CONV_DIM), F32),
            pltpu.VMEM((bt * seq, CONV_DIM), F32),
            pltpu.VMEM((bt * seq, D_INNER), F32),
            pltpu.VMEM((bt * seq, D_INNER), F32),
            pltpu.VMEM((bt * seq, D_INNER), F32),
            pltpu.VMEM((N_HEADS, bt * seq), F32),
            pltpu.VMEM((bt * seq, D_MODEL), BF16),
        ],
        compiler_params=pltpu.CompilerParams(
            dimension_semantics=("arbitrary", "arbitrary"), vmem_limit_bytes=VMEM_LIMIT_BYTES),
        name="ssd_sample",
    )(x, state_conv, state_ssm, *weights, *masks, _head_expander())


def kernel(x_prompt, x_sample, state_pool, state_conv, state_ssm, norm_w, pool_in_w, pool_mix_w, pool_scale,
           pool_out_w, ssd_in_w, ssd_conv_w, ssd_conv_b, ssd_dt_bias, ssd_A_log, ssd_D, ssd_norm_w, ssd_out_w,
           final_norm_w):
    nb, seq, _ = x_sample.shape
    hp = N_HEADS * HEAD_DIM
    xp = x_prompt
    xs = x_sample.reshape(nb * seq, D_MODEL)
    ssm_in = state_ssm.reshape(state_ssm.shape[0], nb, hp, D_STATE)
    fnw = final_norm_w.reshape(1, D_MODEL)
    pool_p, pool_s, conv_p, conv_s, ssm_p, ssm_s = [], [], [], [], [], []
    for i in range(DEPTH):
        j = i // 2
        nw = norm_w[i].reshape(1, D_MODEL)
        if i % 2 == 0:
            win = pool_in_w[j].astype(BF16)
            wmix = pool_mix_w[j].astype(BF16)
            scale = pool_scale[j].reshape(1, D_INNER)
            wout = pool_out_w[j].astype(BF16)
            xp, st = _pool_prompt(xp, nw, win, wmix, scale, wout)
            pool_p.append(st)
            xs, st = _pool_sample(xs, state_pool, j, nw, win, wmix, scale, wout, seq=seq)
            pool_s.append(st[0])
        else:
            w_in = ssd_in_w[j]
            weights = (
                nw,
                w_in[:, :D_INNER].astype(BF16),
                w_in[:, D_INNER:D_INNER + CONV_DIM].astype(BF16),
                w_in[:, D_INNER + CONV_DIM:].astype(BF16),
                ssd_conv_w[j],
                ssd_conv_b[j].reshape(1, CONV_DIM),
                ssd_dt_bias[j].reshape(1, N_HEADS),
                ssd_A_log[j].reshape(1, N_HEADS),
                jnp.repeat(ssd_D[j], HEAD_DIM).reshape(1, D_INNER),
                ssd_norm_w[j].reshape(1, D_INNER),
                ssd_out_w[j].astype(BF16),
                fnw,
            )
            final = i == DEPTH - 1
            xp, cst, sst = _ssd_prompt(xp, weights, final=final)
            conv_p.append(cst)
            ssm_p.append(sst.reshape(-1, N_HEADS, HEAD_DIM, D_STATE))
            xs, cst, sst = _ssd_sample(xs, state_conv, ssm_in, j, weights, seq=seq, final=final)
            conv_s.append(cst[0])
            ssm_s.append(sst[0].reshape(nb, N_HEADS, HEAD_DIM, D_STATE))
    return (xp, xs.reshape(nb, seq, D_MODEL), jnp.stack(pool_p), jnp.stack(pool_s), jnp.stack(conv_p),
            jnp.stack(conv_s), jnp.stack(ssm_p), jnp.stack(ssm_s))
```

```python
import functools

import jax
import jax.numpy as jnp
from jax import lax
from jax.experimental import pallas as pl
from jax.experimental.pallas import tpu as pltpu

D_MODEL = 1024
DEPTH = 4
PAST_LEN = 16384
D_INNER = 2 * D_MODEL
POOL_WINDOWS = (2, 4, 8, 16)
POOL_GROUP = D_INNER // len(POOL_WINDOWS)
POOL_BUF = max(POOL_WINDOWS) - 1
HEAD_DIM = 64
N_HEADS = D_INNER // HEAD_DIM
D_STATE = 128
N_GROUPS = 4
HEADS_PER_GROUP = N_HEADS // N_GROUPS
GROUP_W = HEADS_PER_GROUP * HEAD_DIM
CONV_K = 4
CONV_DIM = D_INNER + 2 * N_GROUPS * D_STATE
CHUNK = 128
EPS = 1e-6

F32 = jnp.float32
BF16 = jnp.bfloat16

SUBLANES = 8
SLAB = 512
VMEM_LIMIT_BYTES = 56 * 1024 * 1024

POOL_TL = 512
SSD_TL = 256
POOL_BT = 32
SSD_BT = 16
SSD_SB = 2


def _const_spec(shape):
    zeros = (0,) * len(shape)
    return pl.BlockSpec(shape, lambda *_: zeros, pipeline_mode=pl.Buffered(1))


def _without_refs(kern, start, count):
    def body(*refs):
        return kern(*refs[:start], *refs[start + count:])
    return body


def _rms(x, w):
    return x * lax.rsqrt(jnp.mean(x * x, axis=-1, keepdims=True) + EPS) * w


def _silu(x):
    half = 0.5 * x
    return half + half * jnp.tanh(half)


def _slabs(w):
    k, n = w.shape
    if n <= SLAB:
        return w.astype(BF16)
    return w.reshape(k, n // SLAB, SLAB).transpose(1, 0, 2).astype(BF16)


def _wcols(ref, c0, width, krows=slice(None)):
    s0, off = divmod(c0, SLAB)
    if width <= SLAB:
        return ref[s0, krows, off:off + width]
    return jnp.concatenate([ref[s0 + i, krows, :] for i in range(width // SLAB)], axis=1)


def _softplus(x):
    return jnp.maximum(x, 0.0) + jnp.log1p(jnp.exp(-jnp.abs(x)))


def _split3(v):
    v1 = v.astype(BF16).astype(F32)
    r1 = v - v1
    v2 = r1.astype(BF16).astype(F32)
    v3 = (r1 - v2).astype(BF16).astype(F32)
    return v1, v2, v3


def _sum3(f, v):
    p1, p2, p3 = (f(p.astype(BF16)) for p in _split3(v))
    return p1 + p2 + p3


def _dot01(m01, v):
    return _sum3(lambda p: jnp.dot(m01, p, preferred_element_type=F32), v)


def _dot01_tn(v, m01):
    dn = (((0,), (0,)), ((), ()))
    return _sum3(lambda p: lax.dot_general(p, m01, dn, preferred_element_type=F32), v)


def _dot01_r(v, m01):
    return _sum3(lambda p: jnp.dot(p, m01, preferred_element_type=F32), v)


def _expand_heads(v, r3):
    parts = jnp.concatenate(_split3(v), axis=1).astype(BF16)
    return jnp.dot(parts, r3, preferred_element_type=F32)


def _pool_tail(p_of_group, z_of_group, wmix_ref, scale_ref, wout_ref):
    acc = None
    for g in range(len(POOL_WINDOWS)):
        cols = slice(g * POOL_GROUP, (g + 1) * POOL_GROUP)
        mixed = jnp.dot(p_of_group(g).astype(BF16), wmix_ref[g], preferred_element_type=F32)
        y = mixed * scale_ref[:, cols] * _silu(z_of_group(g))
        part = jnp.dot(y.astype(BF16), _wcols(wout_ref, 0, D_MODEL, cols), preferred_element_type=F32)
        acc = part if acc is None else acc + part
    return acc


def _pool_prompt_kernel(x_ref, nw_ref, win_ref, wmix_ref, scale_ref, wout_ref, xo_ref, st_ref, ext_ref, *, tl, nt):
    t = pl.program_id(1)
    hist = 2 * SUBLANES
    x = x_ref[0]
    h = _rms(x, nw_ref[...]).astype(BF16)

    @pl.when(t == 0)
    def _():
        ext_ref[0:hist, :] = jnp.zeros((hist, D_INNER), F32)

    ext_ref[hist:hist + tl, :] = jnp.dot(h, _wcols(win_ref, 0, D_INNER), preferred_element_type=F32)
    pos = t * tl + lax.broadcasted_iota(jnp.int32, (tl, 1), 0)

    def p_of_group(g):
        w = POOL_WINDOWS[g]
        cols = slice(g * POOL_GROUP, (g + 1) * POOL_GROUP)
        u = ext_ref[hist:hist + tl, cols]
        s = u
        for k in range(1, w):
            s = s + ext_ref[hist - k:hist - k + tl, cols]
        cnt = jnp.minimum(pos + 1, w).astype(F32)
        return s * (1.0 / cnt) - u

    def z_of_group(g):
        c0 = D_INNER + g * POOL_GROUP
        return jnp.dot(h, _wcols(win_ref, c0, POOL_GROUP), preferred_element_type=F32)

    xo_ref[0] = x + _pool_tail(p_of_group, z_of_group, wmix_ref, scale_ref, wout_ref)

    @pl.when(t == nt - 1)
    def _():
        st_ref[0] = ext_ref[hist + tl - POOL_BUF:hist + tl, :]

    ext_ref[0:hist, :] = ext_ref[tl:tl + hist, :]


def _pool_prompt(x, nw, win, wmix, scale, wout):
    b, l, _ = x.shape
    tl = POOL_TL
    nt = l // tl
    kern = functools.partial(_pool_prompt_kernel, tl=tl, nt=nt)
    return pl.pallas_call(
        kern,
        grid=(b, nt),
        in_specs=[
            pl.BlockSpec((1, tl, D_MODEL), lambda i, t: (i, t, 0)),
            _const_spec((1, D_MODEL)),
            _const_spec((2 * D_INNER // SLAB, D_MODEL, SLAB)),
            _const_spec((len(POOL_WINDOWS), POOL_GROUP, POOL_GROUP)),
            _const_spec((1, D_INNER)),
            _const_spec((D_MODEL // SLAB, D_INNER, SLAB)),
        ],
        out_specs=[
            pl.BlockSpec((1, tl, D_MODEL), lambda i, t: (i, t, 0)),
            pl.BlockSpec((1, POOL_BUF, D_INNER), lambda i, t: (i, 0, 0)),
        ],
        out_shape=[
            jax.ShapeDtypeStruct((b, l, D_MODEL), F32),
            jax.ShapeDtypeStruct((b, POOL_BUF, D_INNER), F32),
        ],
        scratch_shapes=[pltpu.VMEM((tl + 2 * SUBLANES, D_INNER), F32)],
        compiler_params=pltpu.CompilerParams(
            dimension_semantics=("arbitrary", "arbitrary"), vmem_limit_bytes=VMEM_LIMIT_BYTES),
        name="pool_prompt",
    )(x, nw, win, wmix, scale, wout)


def _pool_sample_kernel(x_ref, buf_ref, nw_ref, win_ref, wmix_ref, scale_ref, wout_ref, xo_ref, st_ref, ext_ref,
                        *, bt, seq, start):
    hist = 2 * SUBLANES
    x = x_ref[...]
    h = _rms(x, nw_ref[...]).astype(BF16)
    u = jnp.dot(h, _wcols(win_ref, 0, D_INNER), preferred_element_type=F32)
    ext_ref[:, SUBLANES:hist, :] = jnp.zeros((bt, SUBLANES, D_INNER), F32)
    ext_ref[:, 0:POOL_BUF, :] = buf_ref[0]
    ext_ref[:, hist:hist + seq, :] = u.reshape(bt, seq, D_INNER)
    trow = lax.broadcasted_iota(jnp.int32, (1, seq, 1), 1)

    def p_of_group(g):
        w = POOL_WINDOWS[g]
        cols = slice(g * POOL_GROUP, (g + 1) * POOL_GROUP)
        cur = ext_ref[:, hist:hist + seq, cols]
        s = cur
        for k in range(1, w):
            lower = ext_ref[:, hist - 1 - k:hist - 1 - k + seq, cols]
            if k >= seq:
                term = lower
            else:
                term = jnp.where(trow >= k, ext_ref[:, hist - k:hist - k + seq, cols], lower)
            s = s + term
        cnt = jnp.minimum(start + trow + 1, w).astype(F32)
        return (s * (1.0 / cnt) - cur).reshape(bt * seq, POOL_GROUP)

    def z_of_group(g):
        c0 = D_INNER + g * POOL_GROUP
        return jnp.dot(h, _wcols(win_ref, c0, POOL_GROUP), preferred_element_type=F32)

    xo_ref[...] = x + _pool_tail(p_of_group, z_of_group, wmix_ref, scale_ref, wout_ref)
    keep = POOL_BUF - seq
    st_ref[0, :, 0:keep, :] = ext_ref[:, seq:POOL_BUF, :]
    st_ref[0, :, keep:POOL_BUF, :] = ext_ref[:, hist:hist + seq, :]


def _pool_sample(x, state, layer, nw, win, wmix, scale, wout, prev, *, seq):
    rows = x.shape[0]
    nb = rows // seq
    bt = POOL_BT
    kern = functools.partial(_pool_sample_kernel, bt=bt, seq=seq, start=PAST_LEN)
    operands = [x, state, nw, win, wmix, scale, wout]
    extra_specs, aliases = [], {}
    if prev is not None:
        kern = _without_refs(kern, len(operands), 1)
        extra_specs = [pl.BlockSpec(memory_space=pl.ANY)]
        aliases = {len(operands): 1}
        operands.append(prev)
    return pl.pallas_call(
        kern,
        grid=(nb // bt,),
        in_specs=[
            pl.BlockSpec((bt * seq, D_MODEL), lambda i: (i, 0)),
            pl.BlockSpec((1, bt, POOL_BUF, D_INNER), lambda i: (layer, i, 0, 0)),
            _const_spec((1, D_MODEL)),
            _const_spec((2 * D_INNER // SLAB, D_MODEL, SLAB)),
            _const_spec((len(POOL_WINDOWS), POOL_GROUP, POOL_GROUP)),
            _const_spec((1, D_INNER)),
            _const_spec((D_MODEL // SLAB, D_INNER, SLAB)),
        ] + extra_specs,
        out_specs=[
            pl.BlockSpec((bt * seq, D_MODEL), lambda i: (i, 0)),
            pl.BlockSpec((1, bt, POOL_BUF, D_INNER), lambda i: (layer, i, 0, 0)),
        ],
        out_shape=[
            jax.ShapeDtypeStruct((rows, D_MODEL), F32),
            jax.ShapeDtypeStruct((state.shape[0], nb, POOL_BUF, D_INNER), F32),
        ],
        input_output_aliases=aliases,
        scratch_shapes=[pltpu.VMEM((bt, 2 * SUBLANES + seq, D_INNER), F32)],
        compiler_params=pltpu.CompilerParams(
            dimension_semantics=("arbitrary",), vmem_limit_bytes=VMEM_LIMIT_BYTES),
        name="pool_sample",
    )(*operands)


def _ssd_block_masks(q):
    i = jnp.arange(CHUNK)[:, None]
    j = jnp.arange(CHUNK)[None, :]
    same = (i // q) == (j // q)
    tril = (same & (j <= i)).astype(BF16)
    ones = same.astype(BF16)
    eye = (i == j).astype(BF16)
    return tril, tril.T, ones, eye


def _head_expander():
    h = jnp.arange(N_HEADS)[:, None]
    c = jnp.arange(D_INNER)[None, :] // HEAD_DIM
    r = (h == c).astype(BF16)
    return jnp.concatenate([r, r, r], axis=0)


def _ssd_dense_front(x, nw_ref, wxbc_ref, wdt_ref, dtb_ref, alog_ref):
    h = _rms(x, nw_ref[...]).astype(BF16)
    xbc = jnp.dot(h, _wcols(wxbc_ref, 0, CONV_DIM), preferred_element_type=F32)
    dt = _softplus(jnp.dot(h, wdt_ref[...], preferred_element_type=F32) + dtb_ref[...])
    a = dt * (-jnp.exp(alog_ref[...]))
    return h, xbc, dt, a


def _ssd_block_diag(r0, a_c, dt_c, act_ref, ybuf_ref, tril, triu, ones, eye, q, side_work=None):
    acum = _dot01(tril, a_c)
    acum_t = _dot01_tn(a_c, triu)
    alast = _dot01(ones, a_c)
    dt_t = _dot01_tn(dt_c, eye)
    ii = lax.broadcasted_iota(jnp.int32, (CHUNK, CHUNK), 0)
    jj = lax.broadcasted_iota(jnp.int32, (CHUNK, CHUNK), 1)
    causal = jj <= ii
    if q != CHUNK:
        causal = causal & ((ii // q) == (jj // q))
    rows = slice(r0, r0 + CHUNK)
    first_of_pair = lax.broadcasted_iota(jnp.int32, (CHUNK, 2 * HEAD_DIM), 1) < HEAD_DIM
    for g in range(N_GROUPS):
        if side_work is not None:
            side_work(g)
        b0 = D_INNER + g * D_STATE
        c0 = D_INNER + N_GROUPS * D_STATE + g * D_STATE
        bg = act_ref[rows, b0:b0 + D_STATE].astype(BF16)
        cg = act_ref[rows, c0:c0 + D_STATE].astype(BF16)
        cb = lax.dot_general(cg, bg, (((1,), (1,)), ((), ())), preferred_element_type=F32)
        for pair in range(HEADS_PER_GROUP // 2):
            h0 = g * HEADS_PER_GROUP + 2 * pair
            ws = []
            for hh in (h0, h0 + 1):
                seg = acum[:, hh:hh + 1] - acum_t[hh:hh + 1, :]
                decay = jnp.exp(jnp.where(causal, seg, -jnp.inf))
                ws.append((cb * decay * dt_t[hh:hh + 1, :]).astype(BF16))
            pcols = slice(h0 * HEAD_DIM, (h0 + 2) * HEAD_DIM)
            xp = act_ref[rows, pcols]
            rhs = jnp.concatenate([jnp.where(first_of_pair, xp, 0.0), jnp.where(first_of_pair, 0.0, xp)], axis=0)
            ybuf_ref[rows, pcols] = jnp.dot(jnp.concatenate(ws, axis=1), rhs.astype(BF16),
                                            preferred_element_type=F32)
    return acum, alast


def _ssd_epilogue(x, gate, act_ref, ybuf_ref, dx_ref, normw_ref, wout_ref, fnw_ref, final):
    y = ybuf_ref[...] + dx_ref[...] * act_ref[:, :D_INNER]
    y = _rms(y * gate, normw_ref[...])
    out = x + jnp.dot(y.astype(BF16), _wcols(wout_ref, 0, D_MODEL), preferred_element_type=F32)
    return _rms(out, fnw_ref[...]) if final else out


def _causal_conv(ext, lo, n, cw_ref, cb_ref):
    out = ext(0) * cw_ref[0:1, :]
    for k in range(1, CONV_K):
        out = out + ext(k) * cw_ref[k:k + 1, :]
    return _silu(out + cb_ref[...])


def _ssd_prompt_kernel(x_ref, nw_ref, wz_ref, wxbc_ref, wdt_ref, cw_ref, cb_ref, dtb_ref, alog_ref, dx_ref,
                       normw_ref, wout_ref, fnw_ref, tril_ref, triu_ref, ones_ref, eye_ref, r3_ref,
                       xo_ref, cst_ref, ssm_ref, cext_ref, act_ref, ybuf_ref, ht_ref, gate_ref, *, tl, nt, final):
    t = pl.program_id(1)
    x = x_ref[0]

    @pl.when(t == 0)
    def _():
        cext_ref[0:SUBLANES, :] = jnp.zeros((SUBLANES, CONV_DIM), F32)
        ht_ref[...] = jnp.zeros_like(ht_ref)

    h, xbc, dt, a = _ssd_dense_front(x, nw_ref, wxbc_ref, wdt_ref, dtb_ref, alog_ref)
    cext_ref[SUBLANES:SUBLANES + tl, :] = xbc
    lo = SUBLANES - (CONV_K - 1)
    act_ref[...] = _causal_conv(lambda k: cext_ref[lo + k:lo + k + tl, :], lo, tl, cw_ref, cb_ref)

    tril, triu, ones, eye = tril_ref[...], triu_ref[...], ones_ref[...], eye_ref[...]
    n_chunks = tl // CHUNK
    zw = D_INNER // (n_chunks * N_GROUPS)
    for c in range(n_chunks):
        r0 = c * CHUNK
        rows = slice(r0, r0 + CHUNK)
        a_c, dt_c = a[rows], dt[rows]

        def gate_piece(g, c=c):
            zcols = slice((c * N_GROUPS + g) * zw, (c * N_GROUPS + g + 1) * zw)
            gate_ref[:, zcols] = _silu(jnp.dot(h, _wcols(wz_ref, zcols.start, zw), preferred_element_type=F32))

        acum, alast = _ssd_block_diag(r0, a_c, dt_c, act_ref, ybuf_ref, tril, triu, ones, eye, CHUNK, gate_piece)
        scales = jnp.concatenate(
            [jnp.exp(acum), jnp.exp(alast - acum) * dt_c, jnp.exp(alast[0:SUBLANES])], axis=0)
        scales_x = _expand_heads(scales, r3_ref[...])
        ea_x = scales_x[0:CHUNK]
        dte_x = scales_x[CHUNK:2 * CHUNK]
        cd_x = scales_x[2 * CHUNK:2 * CHUNK + 1]
        for g in range(N_GROUPS):
            gcols = slice(g * GROUP_W, (g + 1) * GROUP_W)
            b0 = D_INNER + g * D_STATE
            c0 = D_INNER + N_GROUPS * D_STATE + g * D_STATE
            bg = act_ref[rows, b0:b0 + D_STATE].astype(BF16)
            cg = act_ref[rows, c0:c0 + D_STATE].astype(BF16)
            hprev = ht_ref[:, gcols]
            y_off = jnp.dot(cg, hprev.astype(BF16), preferred_element_type=F32) * ea_x[:, gcols]
            ybuf_ref[rows, gcols] = ybuf_ref[rows, gcols] + y_off
            xs = (act_ref[rows, gcols] * dte_x[:, gcols]).astype(BF16)
            st = lax.dot_general(bg, xs, (((0,), (0,)), ((), ())), preferred_element_type=F32)
            ht_ref[:, gcols] = hprev * cd_x[:, gcols] + st

    xo_ref[0] = _ssd_epilogue(x, gate_ref[...], act_ref, ybuf_ref, dx_ref, normw_ref, wout_ref, fnw_ref, final)

    @pl.when(t == nt - 1)
    def _():
        cst_ref[0] = cext_ref[SUBLANES + tl - (CONV_K - 1):SUBLANES + tl, :]
        ssm_ref[0] = ht_ref[...].T

    cext_ref[0:SUBLANES, :] = cext_ref[tl:tl + SUBLANES, :]


def _ssd_weight_specs():
    return [
        _const_spec((1, D_MODEL)),
        _const_spec((D_INNER // SLAB, D_MODEL, SLAB)),
        _const_spec((CONV_DIM // SLAB, D_MODEL, SLAB)),
        _const_spec((D_MODEL, N_HEADS)),
        _const_spec((CONV_K, CONV_DIM)),
        _const_spec((1, CONV_DIM)),
        _const_spec((1, N_HEADS)),
        _const_spec((1, N_HEADS)),
        _const_spec((1, D_INNER)),
        _const_spec((1, D_INNER)),
        _const_spec((D_MODEL // SLAB, D_INNER, SLAB)),
        _const_spec((1, D_MODEL)),
        _const_spec((CHUNK, CHUNK)),
        _const_spec((CHUNK, CHUNK)),
        _const_spec((CHUNK, CHUNK)),
        _const_spec((CHUNK, CHUNK)),
        _const_spec((3 * N_HEADS, D_INNER)),
    ]


def _ssd_prompt(x, weights, *, final):
    b, l, _ = x.shape
    tl = SSD_TL
    nt = l // tl
    kern = functools.partial(_ssd_prompt_kernel, tl=tl, nt=nt, final=final)
    masks = _ssd_block_masks(CHUNK)
    return pl.pallas_call(
        kern,
        grid=(b, nt),
        in_specs=[pl.BlockSpec((1, tl, D_MODEL), lambda i, t: (i, t, 0))] + _ssd_weight_specs(),
        out_specs=[
            pl.BlockSpec((1, tl, D_MODEL), lambda i, t: (i, t, 0)),
            pl.BlockSpec((1, CONV_K - 1, CONV_DIM), lambda i, t: (i, 0, 0)),
            pl.BlockSpec((1, N_HEADS * HEAD_DIM, D_STATE), lambda i, t: (i, 0, 0)),
        ],
        out_shape=[
            jax.ShapeDtypeStruct((b, l, D_MODEL), F32),
            jax.ShapeDtypeStruct((b, CONV_K - 1, CONV_DIM), F32),
            jax.ShapeDtypeStruct((b, N_HEADS * HEAD_DIM, D_STATE), F32),
        ],
        scratch_shapes=[
            pltpu.VMEM((tl + SUBLANES, CONV_DIM), F32),
            pltpu.VMEM((tl, CONV_DIM), F32),
            pltpu.VMEM((tl, D_INNER), F32),
            pltpu.VMEM((D_STATE, N_HEADS * HEAD_DIM), F32),
            pltpu.VMEM((tl, D_INNER), F32),
        ],
        compiler_params=pltpu.CompilerParams(
            dimension_semantics=("arbitrary", "arbitrary"), vmem_limit_bytes=VMEM_LIMIT_BYTES),
        name="ssd_prompt",
    )(x, *weights, *masks, _head_expander())


def _ssd_sample_kernel(x_ref, cin_ref, hin_ref, nw_ref, wz_ref, wxbc_ref, wdt_ref, cw_ref, cb_ref, dtb_ref,
                       alog_ref, dx_ref, normw_ref, wout_ref, fnw_ref, tril_ref, triu_ref, ones_ref, eye_ref,
                       r3_ref, xo_ref, cst_ref, hout_ref, cext_ref, act_ref, ybuf_ref, eax_ref, xsc_ref,
                       cdt_ref, h_ref, *, bt, sb, seq, final):
    s_idx = pl.program_id(1)
    n_inner = bt // sb
    rows_total = bt * seq

    @pl.when(s_idx == 0)
    def _():
        x = x_ref[...]
        h, xbc, dt, a = _ssd_dense_front(x, nw_ref, wxbc_ref, wdt_ref, dtb_ref, alog_ref)
        h_ref[...] = h
        lo = SUBLANES - (CONV_K - 1)
        cext_ref[:, 0:SUBLANES, :] = jnp.zeros((bt, SUBLANES, CONV_DIM), F32)
        cext_ref[:, lo:SUBLANES, :] = cin_ref[0]
        cext_ref[:, SUBLANES:SUBLANES + seq, :] = xbc.reshape(bt, seq, CONV_DIM)
        conv = _causal_conv(lambda k: cext_ref[:, lo + k:lo + k + seq, :], lo, seq, cw_ref, cb_ref)
        act_ref[...] = conv.reshape(rows_total, CONV_DIM)
        cst_ref[0] = cext_ref[:, SUBLANES + seq - (CONV_K - 1):SUBLANES + seq, :]
        tril, triu, ones, eye = tril_ref[...], triu_ref[...], ones_ref[...], eye_ref[...]
        for c in range(rows_total // CHUNK):
            r0 = c * CHUNK
            rows = slice(r0, r0 + CHUNK)
            a_c, dt_c = a[rows], dt[rows]
            acum, alast = _ssd_block_diag(r0, a_c, dt_c, act_ref, ybuf_ref, tril, triu, ones, eye, seq)
            scales = jnp.concatenate([jnp.exp(acum), jnp.exp(alast - acum) * dt_c], axis=0)
            scales_x = _expand_heads(scales, r3_ref[...])
            eax_ref[rows, :] = scales_x[0:CHUNK]
            xsc_ref[rows, :] = act_ref[rows, 0:D_INNER] * scales_x[CHUNK:2 * CHUNK]
            cdt_ref[:, rows] = jnp.exp(_dot01_tn(a_c, ones))

    for si in range(sb):
        r0 = pl.multiple_of((s_idx * sb + si) * seq, seq)
        rows = pl.ds(r0, seq)
        pick = (lax.broadcasted_iota(jnp.int32, (rows_total, D_STATE), 0) == r0).astype(BF16)
        cd = _dot01_r(cdt_ref[...], pick)
        for g in range(N_GROUPS):
            gcols = slice(g * GROUP_W, (g + 1) * GROUP_W)
            b0 = D_INNER + g * D_STATE
            c0 = D_INNER + N_GROUPS * D_STATE + g * D_STATE
            bg = act_ref[rows, b0:b0 + D_STATE].astype(BF16)
            cg = act_ref[rows, c0:c0 + D_STATE].astype(BF16)
            hprev = hin_ref[0, si, g * GROUP_W:(g + 1) * GROUP_W, :]
            y_off = lax.dot_general(cg, hprev.astype(BF16), (((1,), (1,)), ((), ())), preferred_element_type=F32)
            ybuf_ref[rows, gcols] = ybuf_ref[rows, gcols] + y_off * eax_ref[rows, gcols]
            st = lax.dot_general(xsc_ref[rows, gcols].astype(BF16), bg, (((0,), (0,)), ((), ())),
                                 preferred_element_type=F32)
            for e in range(HEADS_PER_GROUP):
                hh = g * HEADS_PER_GROUP + e
                hrows = slice(hh * HEAD_DIM, (hh + 1) * HEAD_DIM)
                hout_ref[0, si, hrows, :] = (hin_ref[0, si, hrows, :] * cd[hh:hh + 1, :]
                                             + st[e * HEAD_DIM:(e + 1) * HEAD_DIM, :])

    @pl.when(s_idx == n_inner - 1)
    def _():
        gate = _silu(jnp.dot(h_ref[...], _wcols(wz_ref, 0, D_INNER), preferred_element_type=F32))
        xo_ref[...] = _ssd_epilogue(x_ref[...], gate, act_ref, ybuf_ref, dx_ref, normw_ref, wout_ref, fnw_ref,
                                    final)


def _ssd_sample(x, state_conv, state_ssm, layer, weights, prev, *, seq, final):
    rows = x.shape[0]
    nb = rows // seq
    bt, sb = SSD_BT, SSD_SB
    n_inner = bt // sb
    kern = functools.partial(_ssd_sample_kernel, bt=bt, sb=sb, seq=seq, final=final)
    hp = N_HEADS * HEAD_DIM
    operands = [x, state_conv, state_ssm, *weights, *_ssd_block_masks(seq), _head_expander()]
    extra_specs, aliases = [], {}
    if prev is not None:
        kern = _without_refs(kern, len(operands), len(prev))
        extra_specs = [pl.BlockSpec(memory_space=pl.ANY)] * len(prev)
        aliases = {len(operands): 1, len(operands) + 1: 2}
        operands.extend(prev)
    n_layers = state_ssm.shape[0]
    return pl.pallas_call(
        kern,
        grid=(nb // bt, n_inner),
        in_specs=[
            pl.BlockSpec((bt * seq, D_MODEL), lambda i, s: (i, 0)),
            pl.BlockSpec((1, bt, CONV_K - 1, CONV_DIM), lambda i, s: (layer, i, 0, 0)),
            pl.BlockSpec((1, sb, hp, D_STATE), lambda i, s: (layer, i * n_inner + s, 0, 0)),
        ] + _ssd_weight_specs() + extra_specs,
        out_specs=[
            pl.BlockSpec((bt * seq, D_MODEL), lambda i, s: (i, 0)),
            pl.BlockSpec((1, bt, CONV_K - 1, CONV_DIM), lambda i, s: (layer, i, 0, 0)),
            pl.BlockSpec((1, sb, hp, D_STATE), lambda i, s: (layer, i * n_inner + s, 0, 0)),
        ],
        out_shape=[
            jax.ShapeDtypeStruct((rows, D_MODEL), F32),
            jax.ShapeDtypeStruct((n_layers, nb, CONV_K - 1, CONV_DIM), F32),
            jax.ShapeDtypeStruct((n_layers, nb, hp, D_STATE), F32),
        ],
        input_output_aliases=aliases,
        scratch_shapes=[
            pltpu.VMEM((bt, SUBLANES + seq, CONV_DIM), F32),
            pltpu.VMEM((bt * seq, CONV_DIM), F32),
            pltpu.VMEM((bt * seq, D_INNER), F32),
            pltpu.VMEM((bt * seq, D_INNER), F32),
            pltpu.VMEM((bt * seq, D_INNER), F32),
            pltpu.VMEM((N_HEADS, bt * seq), F32),
            pltpu.VMEM((bt * seq, D_MODEL), BF16),
        ],
        compiler_params=pltpu.CompilerParams(
            dimension_semantics=("arbitrary", "arbitrary"), vmem_limit_bytes=VMEM_LIMIT_BYTES),
        name="ssd_sample",
    )(*operands)


def kernel(x_prompt, x_sample, state_pool, state_conv, state_ssm, norm_w, pool_in_w, pool_mix_w, pool_scale,
           pool_out_w, ssd_in_w, ssd_conv_w, ssd_conv_b, ssd_dt_bias, ssd_A_log, ssd_D, ssd_norm_w, ssd_out_w,
           final_norm_w):
    nb, seq, _ = x_sample.shape
    hp = N_HEADS * HEAD_DIM
    xp = x_prompt
    xs = x_sample.reshape(nb * seq, D_MODEL)
    ssm_in = state_ssm.reshape(state_ssm.shape[0], nb, hp, D_STATE)
    fnw = final_norm_w.reshape(1, D_MODEL)
    pool_p, conv_p, ssm_p = [], [], []
    pool_s = ssd_s = None
    for i in range(DEPTH):
        j = i // 2
        nw = norm_w[i].reshape(1, D_MODEL)
        if i % 2 == 0:
            win = _slabs(pool_in_w[j])
            wmix = pool_mix_w[j].astype(BF16)
            scale = pool_scale[j].reshape(1, D_INNER)
            wout = _slabs(pool_out_w[j])
            xp, st = _pool_prompt(xp, nw, win, wmix, scale, wout)
            pool_p.append(st)
            xs, pool_s = _pool_sample(xs, state_pool, j, nw, win, wmix, scale, wout, pool_s, seq=seq)
        else:
            w_in = ssd_in_w[j]
            weights = (
                nw,
                _slabs(w_in[:, :D_INNER]),
                _slabs(w_in[:, D_INNER:D_INNER + CONV_DIM]),
                _slabs(w_in[:, D_INNER + CONV_DIM:]),
                ssd_conv_w[j],
                ssd_conv_b[j].reshape(1, CONV_DIM),
                ssd_dt_bias[j].reshape(1, N_HEADS),
                ssd_A_log[j].reshape(1, N_HEADS),
                jnp.repeat(ssd_D[j], HEAD_DIM).reshape(1, D_INNER),
                ssd_norm_w[j].reshape(1, D_INNER),
                _slabs(ssd_out_w[j]),
                fnw,
            )
            final = i == DEPTH - 1
            xp, cst, sst = _ssd_prompt(xp, weights, final=final)
            conv_p.append(cst)
            ssm_p.append(sst.reshape(-1, N_HEADS, HEAD_DIM, D_STATE))
            xs, *ssd_s = _ssd_sample(xs, state_conv, ssm_in, j, weights, ssd_s, seq=seq, final=final)
    conv_s, ssm_s = ssd_s
    return (xp, xs.reshape(nb, seq, D_MODEL), jnp.stack(pool_p), pool_s, jnp.stack(conv_p), conv_s,
            jnp.stack(ssm_p), ssm_s.reshape(state_ssm.shape))
```

```python
import functools

import jax
import jax.numpy as jnp
from jax import lax
from jax.experimental import pallas as pl
from jax.experimental.pallas import tpu as pltpu

D_MODEL = 1024
DEPTH = 4
PAST_LEN = 16384
D_INNER = 2 * D_MODEL
POOL_WINDOWS = (2, 4, 8, 16)
POOL_GROUP = D_INNER // len(POOL_WINDOWS)
POOL_BUF = max(POOL_WINDOWS) - 1
HEAD_DIM = 64
N_HEADS = D_INNER // HEAD_DIM
D_STATE = 128
N_GROUPS = 4
HEADS_PER_GROUP = N_HEADS // N_GROUPS
GROUP_W = HEADS_PER_GROUP * HEAD_DIM
CONV_K = 4
CONV_DIM = D_INNER + 2 * N_GROUPS * D_STATE
CHUNK = 128
EPS = 1e-6

F32 = jnp.float32
BF16 = jnp.bfloat16

SUBLANES = 8
LANES = 128
SLAB = 512
VMEM_LIMIT_BYTES = 56 * 1024 * 1024

N_WIN = 2 * D_INNER // SLAB
N_WOUT = D_MODEL // SLAB
N_WZ = D_INNER // SLAB
N_WXBC = CONV_DIM // SLAB
DT_LANE_BLOCK = (D_INNER + CONV_DIM) // LANES

POOL_TL = 512
SSD_TL = 256
POOL_BT = 32
SSD_BT = 16
SSD_SB = 2


def _const_spec(shape):
    zeros = (0,) * len(shape)
    return pl.BlockSpec(shape, lambda *_: zeros, pipeline_mode=pl.Buffered(1))


def _layer_spec(shape, layer):
    idx = (layer,) + (0,) * len(shape)
    return pl.BlockSpec((None,) + tuple(shape), lambda *_: idx, pipeline_mode=pl.Buffered(1))


def _slab_specs(rows, layer, first, count):
    def spec(c):
        return pl.BlockSpec((None, rows, SLAB), lambda *_: (layer, 0, c), pipeline_mode=pl.Buffered(1))
    return [spec(first + c) for c in range(count)]


def _split(refs, *counts):
    out, pos = [], 0
    for n in counts:
        if n is None:
            out.append(refs[pos])
            pos += 1
        else:
            out.append(refs[pos:pos + n])
            pos += n
    assert pos == len(refs)
    return out


def _wcols(slabs, c0, width, krows=slice(None)):
    s0, off = divmod(c0, SLAB)
    if width <= SLAB:
        return slabs[s0][krows, off:off + width]
    return jnp.concatenate([slabs[s0 + i][krows, :] for i in range(width // SLAB)], axis=1)


def _rms(x, w):
    return x * lax.rsqrt(jnp.mean(x * x, axis=-1, keepdims=True) + EPS) * w


def _silu(x):
    half = 0.5 * x
    return half + half * jnp.tanh(half)


def _softplus(x):
    return jnp.maximum(x, 0.0) + jnp.log1p(jnp.exp(-jnp.abs(x)))


def _split3(v):
    v1 = v.astype(BF16).astype(F32)
    r1 = v - v1
    v2 = r1.astype(BF16).astype(F32)
    v3 = (r1 - v2).astype(BF16).astype(F32)
    return v1, v2, v3


def _sum3(f, v):
    p1, p2, p3 = (f(p.astype(BF16)) for p in _split3(v))
    return p1 + p2 + p3


def _dot01(m01, v):
    return _sum3(lambda p: jnp.dot(m01, p, preferred_element_type=F32), v)


def _dot01_tn(v, m01):
    dn = (((0,), (0,)), ((), ()))
    return _sum3(lambda p: lax.dot_general(p, m01, dn, preferred_element_type=F32), v)


def _dot01_r(v, m01):
    return _sum3(lambda p: jnp.dot(p, m01, preferred_element_type=F32), v)


def _expand_heads(v, r3):
    parts = jnp.concatenate(_split3(v), axis=1).astype(BF16)
    return jnp.dot(parts, r3, preferred_element_type=F32)


_POOL_WEIGHT_COUNTS = (None, N_WIN, None, None, N_WOUT)


def _pool_weight_operands(params, i, j):
    norm_w, win, wmix, scale, wout = params
    operands = [norm_w] + [win] * N_WIN + [wmix, scale] + [wout] * N_WOUT
    specs = ([_const_spec(norm_w.shape)] + _slab_specs(D_MODEL, j, 0, N_WIN)
             + [_layer_spec(wmix.shape[1:], j), _const_spec(scale.shape)] + _slab_specs(D_INNER, j, 0, N_WOUT))
    return operands, specs


def _pool_tail(p_of_group, z_of_group, wmix_ref, scale, wout):
    acc = None
    for g in range(len(POOL_WINDOWS)):
        cols = slice(g * POOL_GROUP, (g + 1) * POOL_GROUP)
        mixed = jnp.dot(p_of_group(g).astype(BF16), wmix_ref[g], preferred_element_type=F32)
        y = mixed * scale[:, cols] * _silu(z_of_group(g))
        part = jnp.dot(y.astype(BF16), _wcols(wout, 0, D_MODEL, cols), preferred_element_type=F32)
        acc = part if acc is None else acc + part
    return acc


def _pool_prompt_kernel(*refs, tl, nt, i, j):
    x_ref, nw_ref, win, wmix_ref, scale_ref, wout, xo_ref, st_ref, ext_ref = _split(
        refs, None, *_POOL_WEIGHT_COUNTS, None, None, None)
    t = pl.program_id(1)
    hist = 2 * SUBLANES
    x = x_ref[0]
    h = _rms(x, nw_ref[i:i + 1, :]).astype(BF16)

    @pl.when(t == 0)
    def _():
        ext_ref[0:hist, :] = jnp.zeros((hist, D_INNER), F32)

    ext_ref[hist:hist + tl, :] = jnp.dot(h, _wcols(win, 0, D_INNER), preferred_element_type=F32)
    pos = t * tl + lax.broadcasted_iota(jnp.int32, (tl, 1), 0)

    def p_of_group(g):
        w = POOL_WINDOWS[g]
        cols = slice(g * POOL_GROUP, (g + 1) * POOL_GROUP)
        u = ext_ref[hist:hist + tl, cols]
        s = u
        for k in range(1, w):
            s = s + ext_ref[hist - k:hist - k + tl, cols]
        cnt = jnp.minimum(pos + 1, w).astype(F32)
        return s * (1.0 / cnt) - u

    def z_of_group(g):
        return jnp.dot(h, _wcols(win, D_INNER + g * POOL_GROUP, POOL_GROUP), preferred_element_type=F32)

    xo_ref[0] = x + _pool_tail(p_of_group, z_of_group, wmix_ref, scale_ref[j:j + 1, :], wout)

    @pl.when(t == nt - 1)
    def _():
        st_ref[0] = ext_ref[hist + tl - POOL_BUF:hist + tl, :]

    ext_ref[0:hist, :] = ext_ref[tl:tl + hist, :]


def _pool_prompt(x, params, i, j):
    b, l, _ = x.shape
    tl = POOL_TL
    nt = l // tl
    w_operands, w_specs = _pool_weight_operands(params, i, j)
    return pl.pallas_call(
        functools.partial(_pool_prompt_kernel, tl=tl, nt=nt, i=i, j=j),
        grid=(b, nt),
        in_specs=[pl.BlockSpec((1, tl, D_MODEL), lambda s, t: (s, t, 0))] + w_specs,
        out_specs=[
            pl.BlockSpec((1, tl, D_MODEL), lambda s, t: (s, t, 0)),
            pl.BlockSpec((1, POOL_BUF, D_INNER), lambda s, t: (s, 0, 0)),
        ],
        out_shape=[
            jax.ShapeDtypeStruct((b, l, D_MODEL), F32),
            jax.ShapeDtypeStruct((b, POOL_BUF, D_INNER), F32),
        ],
        scratch_shapes=[pltpu.VMEM((tl + 2 * SUBLANES, D_INNER), F32)],
        compiler_params=pltpu.CompilerParams(
            dimension_semantics=("arbitrary", "arbitrary"), vmem_limit_bytes=VMEM_LIMIT_BYTES),
        name="pool_prompt",
    )(x, *w_operands)


def _pool_sample_kernel(*refs, bt, seq, start, i, j, n_prev):
    x_ref, buf_ref, nw_ref, win, wmix_ref, scale_ref, wout, _, xo_ref, st_ref, ext_ref = _split(
        refs, None, None, *_POOL_WEIGHT_COUNTS, n_prev, None, None, None)
    hist = 2 * SUBLANES
    x = x_ref[...]
    h = _rms(x, nw_ref[i:i + 1, :]).astype(BF16)
    u = jnp.dot(h, _wcols(win, 0, D_INNER), preferred_element_type=F32)
    ext_ref[:, SUBLANES:hist, :] = jnp.zeros((bt, SUBLANES, D_INNER), F32)
    ext_ref[:, 0:POOL_BUF, :] = buf_ref[0]
    ext_ref[:, hist:hist + seq, :] = u.reshape(bt, seq, D_INNER)
    trow = lax.broadcasted_iota(jnp.int32, (1, seq, 1), 1)

    def p_of_group(g):
        w = POOL_WINDOWS[g]
        cols = slice(g * POOL_GROUP, (g + 1) * POOL_GROUP)
        cur = ext_ref[:, hist:hist + seq, cols]
        s = cur
        for k in range(1, w):
            lower = ext_ref[:, hist - 1 - k:hist - 1 - k + seq, cols]
            if k >= seq:
                term = lower
            else:
                term = jnp.where(trow >= k, ext_ref[:, hist - k:hist - k + seq, cols], lower)
            s = s + term
        cnt = jnp.minimum(start + trow + 1, w).astype(F32)
        return (s * (1.0 / cnt) - cur).reshape(bt * seq, POOL_GROUP)

    def z_of_group(g):
        return jnp.dot(h, _wcols(win, D_INNER + g * POOL_GROUP, POOL_GROUP), preferred_element_type=F32)

    xo_ref[...] = x + _pool_tail(p_of_group, z_of_group, wmix_ref, scale_ref[j:j + 1, :], wout)
    keep = POOL_BUF - seq
    st_ref[0, :, 0:keep, :] = ext_ref[:, seq:POOL_BUF, :]
    st_ref[0, :, keep:POOL_BUF, :] = ext_ref[:, hist:hist + seq, :]


def _pool_sample(x, state, params, i, j, prev, *, seq):
    rows = x.shape[0]
    nb = rows // seq
    bt = POOL_BT
    w_operands, w_specs = _pool_weight_operands(params, i, j)
    operands = [x, state] + w_operands
    prev = [] if prev is None else [prev]
    aliases = {len(operands) + k: 1 + k for k in range(len(prev))}
    return pl.pallas_call(
        functools.partial(_pool_sample_kernel, bt=bt, seq=seq, start=PAST_LEN, i=i, j=j, n_prev=len(prev)),
        grid=(nb // bt,),
        in_specs=[
            pl.BlockSpec((bt * seq, D_MODEL), lambda s: (s, 0)),
            pl.BlockSpec((1, bt, POOL_BUF, D_INNER), lambda s: (j, s, 0, 0)),
        ] + w_specs + [pl.BlockSpec(memory_space=pl.ANY)] * len(prev),
        out_specs=[
            pl.BlockSpec((bt * seq, D_MODEL), lambda s: (s, 0)),
            pl.BlockSpec((1, bt, POOL_BUF, D_INNER), lambda s: (j, s, 0, 0)),
        ],
        out_shape=[
            jax.ShapeDtypeStruct((rows, D_MODEL), F32),
            jax.ShapeDtypeStruct((state.shape[0], nb, POOL_BUF, D_INNER), F32),
        ],
        input_output_aliases=aliases,
        scratch_shapes=[pltpu.VMEM((bt, 2 * SUBLANES + seq, D_INNER), F32)],
        compiler_params=pltpu.CompilerParams(
            dimension_semantics=("arbitrary",), vmem_limit_bytes=VMEM_LIMIT_BYTES),
        name="pool_sample",
    )(*operands, *prev)


_SSD_WEIGHT_COUNTS = (None, N_WZ, N_WXBC) + (None,) * 7 + (N_WOUT,) + (None,) * 6


def _ssd_block_masks(q):
    i = jnp.arange(CHUNK)[:, None]
    j = jnp.arange(CHUNK)[None, :]
    same = (i // q) == (j // q)
    tril = (same & (j <= i)).astype(BF16)
    ones = same.astype(BF16)
    eye = (i == j).astype(BF16)
    return [tril, tril.T, ones, eye]


def _head_expander():
    h = jnp.arange(N_HEADS)[:, None]
    c = jnp.arange(D_INNER)[None, :] // HEAD_DIM
    r = (h == c).astype(BF16)
    return jnp.concatenate([r, r, r], axis=0)


def _ssd_weight_operands(params, j, q):
    norm_w, w_in, conv_w, conv_b, dt_bias, a_log, d_cols, ssd_norm_w, w_out, final_norm_w = params
    consts = _ssd_block_masks(q) + [_head_expander()]
    small = [conv_w, conv_b, dt_bias, a_log, d_cols, ssd_norm_w]
    operands = ([norm_w] + [w_in] * (N_WZ + N_WXBC + 1) + small + [w_out] * N_WOUT + [final_norm_w] + consts)
    dt_spec = pl.BlockSpec((None, D_MODEL, LANES), lambda *_: (j, 0, DT_LANE_BLOCK), pipeline_mode=pl.Buffered(1))
    specs = ([_const_spec(norm_w.shape)] + _slab_specs(D_MODEL, j, 0, N_WZ) + _slab_specs(D_MODEL, j, N_WZ, N_WXBC)
             + [dt_spec] + [_const_spec(a.shape) for a in small] + _slab_specs(D_INNER, j, 0, N_WOUT)
             + [_const_spec(final_norm_w.shape)] + [_const_spec(c.shape) for c in consts])
    return operands, specs


def _dt_and_decay(dt_raw, dtb_ref, alog_ref, j):
    dt = _softplus(dt_raw + dtb_ref[j:j + 1, :])
    return dt, dt * (-jnp.exp(alog_ref[j:j + 1, :]))


def _ssd_block_diag(r0, a_c, dt_c, act_ref, ybuf_ref, tril, triu, ones, eye, q, side_work=None):
    acum = _dot01(tril, a_c)
    acum_t = _dot01_tn(a_c, triu)
    alast = _dot01(ones, a_c)
    dt_t = _dot01_tn(dt_c, eye)
    ii = lax.broadcasted_iota(jnp.int32, (CHUNK, CHUNK), 0)
    jj = lax.broadcasted_iota(jnp.int32, (CHUNK, CHUNK), 1)
    causal = jj <= ii
    if q != CHUNK:
        causal = causal & ((ii // q) == (jj // q))
    rows = slice(r0, r0 + CHUNK)
    first_of_pair = lax.broadcasted_iota(jnp.int32, (CHUNK, 2 * HEAD_DIM), 1) < HEAD_DIM
    for g in range(N_GROUPS):
        if side_work is not None:
            side_work(g)
        b0 = D_INNER + g * D_STATE
        c0 = D_INNER + N_GROUPS * D_STATE + g * D_STATE
        bg = act_ref[rows, b0:b0 + D_STATE].astype(BF16)
        cg = act_ref[rows, c0:c0 + D_STATE].astype(BF16)
        cb = lax.dot_general(cg, bg, (((1,), (1,)), ((), ())), preferred_element_type=F32)
        for pair in range(HEADS_PER_GROUP // 2):
            h0 = g * HEADS_PER_GROUP + 2 * pair
            ws = []
            for hh in (h0, h0 + 1):
                seg = acum[:, hh:hh + 1] - acum_t[hh:hh + 1, :]
                decay = jnp.exp(jnp.where(causal, seg, -jnp.inf))
                ws.append((cb * decay * dt_t[hh:hh + 1, :]).astype(BF16))
            pcols = slice(h0 * HEAD_DIM, (h0 + 2) * HEAD_DIM)
            xp = act_ref[rows, pcols]
            rhs = jnp.concatenate([jnp.where(first_of_pair, xp, 0.0), jnp.where(first_of_pair, 0.0, xp)], axis=0)
            ybuf_ref[rows, pcols] = jnp.dot(jnp.concatenate(ws, axis=1), rhs.astype(BF16),
                                            preferred_element_type=F32)
    return acum, alast


def _ssd_epilogue(x, gate, act_ref, ybuf_ref, d_cols, norm_w, wout, fnw_ref, final):
    y = ybuf_ref[...] + d_cols * act_ref[:, :D_INNER]
    y = _rms(y * gate, norm_w)
    out = x + jnp.dot(y.astype(BF16), _wcols(wout, 0, D_MODEL), preferred_element_type=F32)
    return _rms(out, fnw_ref[...]) if final else out


def _causal_conv(ext, cw_ref, cb_ref, j):
    out = ext(0) * cw_ref[j, 0:1, :]
    for k in range(1, CONV_K):
        out = out + ext(k) * cw_ref[j, k:k + 1, :]
    return _silu(out + cb_ref[j:j + 1, :])


def _ssd_prompt_kernel(*refs, tl, nt, i, j, final):
    (x_ref, nw_ref, wz, wxbc, wdt_ref, cw_ref, cb_ref, dtb_ref, alog_ref, dx_ref, normw_ref, wout, fnw_ref,
     tril_ref, triu_ref, ones_ref, eye_ref, r3_ref, xo_ref, cst_ref, ssm_ref,
     cext_ref, act_ref, ybuf_ref, ht_ref, gate_ref) = _split(refs, None, *_SSD_WEIGHT_COUNTS, *(None,) * 8)
    t = pl.program_id(1)

    @pl.when(t == 0)
    def _():
        cext_ref[0:SUBLANES, :] = jnp.zeros((SUBLANES, CONV_DIM), F32)
        ht_ref[...] = jnp.zeros_like(ht_ref)

    h = _rms(x_ref[0], nw_ref[i:i + 1, :]).astype(BF16)
    cext_ref[SUBLANES:SUBLANES + tl, :] = jnp.dot(h, _wcols(wxbc, 0, CONV_DIM), preferred_element_type=F32)
    dt, a = _dt_and_decay(jnp.dot(h, wdt_ref[:, 0:N_HEADS], preferred_element_type=F32), dtb_ref, alog_ref, j)
    lo = SUBLANES - (CONV_K - 1)
    act_ref[...] = _causal_conv(lambda k: cext_ref[lo + k:lo + k + tl, :], cw_ref, cb_ref, j)

    tril, triu, ones, eye = tril_ref[...], triu_ref[...], ones_ref[...], eye_ref[...]
    n_chunks = tl // CHUNK
    zw = D_INNER // (n_chunks * N_GROUPS)
    for c in range(n_chunks):
        r0 = c * CHUNK
        rows = slice(r0, r0 + CHUNK)
        a_c, dt_c = a[rows], dt[rows]

        def gate_piece(g, c=c):
            z0 = (c * N_GROUPS + g) * zw
            gate_ref[:, z0:z0 + zw] = _silu(jnp.dot(h, _wcols(wz, z0, zw), preferred_element_type=F32))

        acum, alast = _ssd_block_diag(r0, a_c, dt_c, act_ref, ybuf_ref, tril, triu, ones, eye, CHUNK, gate_piece)
        scales = jnp.concatenate(
            [jnp.exp(acum), jnp.exp(alast - acum) * dt_c, jnp.exp(alast[0:SUBLANES])], axis=0)
        for g in range(N_GROUPS):
            gcols = slice(g * GROUP_W, (g + 1) * GROUP_W)
            scales_x = _expand_heads(scales, r3_ref[:, gcols])
            ea_x = scales_x[0:CHUNK]
            dte_x = scales_x[CHUNK:2 * CHUNK]
            cd_x = scales_x[2 * CHUNK:2 * CHUNK + 1]
            b0 = D_INNER + g * D_STATE
            c0 = D_INNER + N_GROUPS * D_STATE + g * D_STATE
            bg = act_ref[rows, b0:b0 + D_STATE].astype(BF16)
            cg = act_ref[rows, c0:c0 + D_STATE].astype(BF16)
            hprev = ht_ref[:, gcols]
            y_off = jnp.dot(cg, hprev.astype(BF16), preferred_element_type=F32) * ea_x
            ybuf_ref[rows, gcols] = ybuf_ref[rows, gcols] + y_off
            xs = (act_ref[rows, gcols] * dte_x).astype(BF16)
            st = lax.dot_general(bg, xs, (((0,), (0,)), ((), ())), preferred_element_type=F32)
            ht_ref[:, gcols] = hprev * cd_x + st

    xo_ref[0] = _ssd_epilogue(x_ref[0], gate_ref[...], act_ref, ybuf_ref, dx_ref[j:j + 1, :], normw_ref[j:j + 1, :],
                              wout, fnw_ref, final)

    @pl.when(t == nt - 1)
    def _():
        cst_ref[0] = cext_ref[SUBLANES + tl - (CONV_K - 1):SUBLANES + tl, :]
        ssm_ref[0] = ht_ref[...].T

    cext_ref[0:SUBLANES, :] = cext_ref[tl:tl + SUBLANES, :]


def _ssd_prompt(x, params, i, j, *, final):
    b, l, _ = x.shape
    tl = SSD_TL
    nt = l // tl
    w_operands, w_specs = _ssd_weight_operands(params, j, CHUNK)
    return pl.pallas_call(
        functools.partial(_ssd_prompt_kernel, tl=tl, nt=nt, i=i, j=j, final=final),
        grid=(b, nt),
        in_specs=[pl.BlockSpec((1, tl, D_MODEL), lambda s, t: (s, t, 0))] + w_specs,
        out_specs=[
            pl.BlockSpec((1, tl, D_MODEL), lambda s, t: (s, t, 0)),
            pl.BlockSpec((1, CONV_K - 1, CONV_DIM), lambda s, t: (s, 0, 0)),
            pl.BlockSpec((1, N_HEADS * HEAD_DIM, D_STATE), lambda s, t: (s, 0, 0)),
        ],
        out_shape=[
            jax.ShapeDtypeStruct((b, l, D_MODEL), F32),
            jax.ShapeDtypeStruct((b, CONV_K - 1, CONV_DIM), F32),
            jax.ShapeDtypeStruct((b, N_HEADS * HEAD_DIM, D_STATE), F32),
        ],
        scratch_shapes=[
            pltpu.VMEM((tl + SUBLANES, CONV_DIM), F32),
            pltpu.VMEM((tl, CONV_DIM), F32),
            pltpu.VMEM((tl, D_INNER), F32),
            pltpu.VMEM((D_STATE, N_HEADS * HEAD_DIM), F32),
            pltpu.VMEM((tl, D_INNER), F32),
        ],
        compiler_params=pltpu.CompilerParams(
            dimension_semantics=("arbitrary", "arbitrary"), vmem_limit_bytes=VMEM_LIMIT_BYTES),
        name="ssd_prompt",
    )(x, *w_operands)


def _ssd_sample_kernel(*refs, bt, sb, seq, i, j, final, n_prev):
    (x_ref, cin_ref, hin_ref, nw_ref, wz, wxbc, wdt_ref, cw_ref, cb_ref, dtb_ref, alog_ref, dx_ref, normw_ref, wout,
     fnw_ref, tril_ref, triu_ref, ones_ref, eye_ref, r3_ref, _, xo_ref, cst_ref, hout_ref,
     cext_ref, act_ref, ybuf_ref, eax_ref, xsc_ref, cdt_ref, h_ref) = _split(
         refs, None, None, None, *_SSD_WEIGHT_COUNTS, n_prev, *(None,) * 10)
    s_idx = pl.program_id(1)
    n_inner = bt // sb
    rows_total = bt * seq

    @pl.when(s_idx == 0)
    def _():
        h = _rms(x_ref[...], nw_ref[i:i + 1, :]).astype(BF16)
        h_ref[...] = h
        xbc = jnp.dot(h, _wcols(wxbc, 0, CONV_DIM), preferred_element_type=F32)
        dt, a = _dt_and_decay(jnp.dot(h, wdt_ref[:, 0:N_HEADS], preferred_element_type=F32), dtb_ref, alog_ref, j)
        lo = SUBLANES - (CONV_K - 1)
        cext_ref[:, 0:SUBLANES, :] = jnp.zeros((bt, SUBLANES, CONV_DIM), F32)
        cext_ref[:, lo:SUBLANES, :] = cin_ref[0]
        cext_ref[:, SUBLANES:SUBLANES + seq, :] = xbc.reshape(bt, seq, CONV_DIM)
        conv = _causal_conv(lambda k: cext_ref[:, lo + k:lo + k + seq, :], cw_ref, cb_ref, j)
        act_ref[...] = conv.reshape(rows_total, CONV_DIM)
        cst_ref[0] = cext_ref[:, SUBLANES + seq - (CONV_K - 1):SUBLANES + seq, :]
        tril, triu, ones, eye = tril_ref[...], triu_ref[...], ones_ref[...], eye_ref[...]
        for c in range(rows_total // CHUNK):
            r0 = c * CHUNK
            rows = slice(r0, r0 + CHUNK)
            a_c, dt_c = a[rows], dt[rows]
            acum, alast = _ssd_block_diag(r0, a_c, dt_c, act_ref, ybuf_ref, tril, triu, ones, eye, seq)
            scales = jnp.concatenate([jnp.exp(acum), jnp.exp(alast - acum) * dt_c], axis=0)
            scales_x = _expand_heads(scales, r3_ref[...])
            eax_ref[rows, :] = scales_x[0:CHUNK]
            xsc_ref[rows, :] = act_ref[rows, 0:D_INNER] * scales_x[CHUNK:2 * CHUNK]
            cdt_ref[:, rows] = jnp.exp(_dot01_tn(a_c, ones))

    for si in range(sb):
        r0 = pl.multiple_of((s_idx * sb + si) * seq, seq)
        rows = pl.ds(r0, seq)
        pick = (lax.broadcasted_iota(jnp.int32, (rows_total, D_STATE), 0) == r0).astype(BF16)
        cd = _dot01_r(cdt_ref[...], pick)
        for g in range(N_GROUPS):
            gcols = slice(g * GROUP_W, (g + 1) * GROUP_W)
            b0 = D_INNER + g * D_STATE
            c0 = D_INNER + N_GROUPS * D_STATE + g * D_STATE
            bg = act_ref[rows, b0:b0 + D_STATE].astype(BF16)
            cg = act_ref[rows, c0:c0 + D_STATE].astype(BF16)
            hprev = hin_ref[0, si, g * GROUP_W:(g + 1) * GROUP_W, :]
            y_off = lax.dot_general(cg, hprev.astype(BF16), (((1,), (1,)), ((), ())), preferred_element_type=F32)
            ybuf_ref[rows, gcols] = ybuf_ref[rows, gcols] + y_off * eax_ref[rows, gcols]
            st = lax.dot_general(xsc_ref[rows, gcols].astype(BF16), bg, (((0,), (0,)), ((), ())),
                                 preferred_element_type=F32)
            for e in range(HEADS_PER_GROUP):
                hh = g * HEADS_PER_GROUP + e
                hrows = slice(hh * HEAD_DIM, (hh + 1) * HEAD_DIM)
                hout_ref[0, si, hrows, :] = (hin_ref[0, si, hrows, :] * cd[hh:hh + 1, :]
                                             + st[e * HEAD_DIM:(e + 1) * HEAD_DIM, :])

    @pl.when(s_idx == n_inner - 1)
    def _():
        gate = _silu(jnp.dot(h_ref[...], _wcols(wz, 0, D_INNER), preferred_element_type=F32))
        xo_ref[...] = _ssd_epilogue(x_ref[...], gate, act_ref, ybuf_ref, dx_ref[j:j + 1, :], normw_ref[j:j + 1, :],
                                    wout, fnw_ref, final)


def _ssd_sample(x, state_conv, state_ssm, params, i, j, prev, *, seq, final):
    rows = x.shape[0]
    nb = rows // seq
    bt, sb = SSD_BT, SSD_SB
    n_inner = bt // sb
    hp = N_HEADS * HEAD_DIM
    w_operands, w_specs = _ssd_weight_operands(params, j, seq)
    operands = [x, state_conv, state_ssm] + w_operands
    prev = [] if prev is None else list(prev)
    aliases = {len(operands) + k: 1 + k for k in range(len(prev))}
    n_layers = state_ssm.shape[0]
    return pl.pallas_call(
        functools.partial(_ssd_sample_kernel, bt=bt, sb=sb, seq=seq, i=i, j=j, final=final, n_prev=len(prev)),
        grid=(nb // bt, n_inner),
        in_specs=[
            pl.BlockSpec((bt * seq, D_MODEL), lambda b, s: (b, 0)),
            pl.BlockSpec((1, bt, CONV_K - 1, CONV_DIM), lambda b, s: (j, b, 0, 0)),
            pl.BlockSpec((1, sb, hp, D_STATE), lambda b, s: (j, b * n_inner + s, 0, 0)),
        ] + w_specs + [pl.BlockSpec(memory_space=pl.ANY)] * len(prev),
        out_specs=[
            pl.BlockSpec((bt * seq, D_MODEL), lambda b, s: (b, 0)),
            pl.BlockSpec((1, bt, CONV_K - 1, CONV_DIM), lambda b, s: (j, b, 0, 0)),
            pl.BlockSpec((1, sb, hp, D_STATE), lambda b, s: (j, b * n_inner + s, 0, 0)),
        ],
        out_shape=[
            jax.ShapeDtypeStruct((rows, D_MODEL), F32),
            jax.ShapeDtypeStruct((n_layers, nb, CONV_K - 1, CONV_DIM), F32),
            jax.ShapeDtypeStruct((n_layers, nb, hp, D_STATE), F32),
        ],
        input_output_aliases=aliases,
        scratch_shapes=[
            pltpu.VMEM((bt, SUBLANES + seq, CONV_DIM), F32),
            pltpu.VMEM((bt * seq, CONV_DIM), F32),
            pltpu.VMEM((bt * seq, D_INNER), F32),
            pltpu.VMEM((bt * seq, D_INNER), F32),
            pltpu.VMEM((bt * seq, D_INNER), F32),
            pltpu.VMEM((N_HEADS, bt * seq), F32),
            pltpu.VMEM((bt * seq, D_MODEL), BF16),
        ],
        compiler_params=pltpu.CompilerParams(
            dimension_semantics=("arbitrary", "arbitrary"), vmem_limit_bytes=VMEM_LIMIT_BYTES),
        name="ssd_sample",
    )(*operands, *prev)


def kernel(x_prompt, x_sample, state_pool, state_conv, state_ssm, norm_w, pool_in_w, pool_mix_w, pool_scale,
           pool_out_w, ssd_in_w, ssd_conv_w, ssd_conv_b, ssd_dt_bias, ssd_A_log, ssd_D, ssd_norm_w, ssd_out_w,
           final_norm_w):
    nb, seq, _ = x_sample.shape
    hp = N_HEADS * HEAD_DIM
    xp = x_prompt
    xs = x_sample.reshape(nb * seq, D_MODEL)
    ssm_in = state_ssm.reshape(state_ssm.shape[0], nb, hp, D_STATE)
    pool_params = (norm_w, pool_in_w.astype(BF16), pool_mix_w.astype(BF16), pool_scale, pool_out_w.astype(BF16))
    ssd_params = (norm_w, ssd_in_w.astype(BF16), ssd_conv_w, ssd_conv_b, ssd_dt_bias, ssd_A_log,
                  jnp.repeat(ssd_D, HEAD_DIM, axis=1), ssd_norm_w, ssd_out_w.astype(BF16),
                  final_norm_w.reshape(1, D_MODEL))
    pool_p, conv_p, ssm_p = [], [], []
    pool_s = ssd_s = None
    for i in range(DEPTH):
        j = i // 2
        if i % 2 == 0:
            xp, st = _pool_prompt(xp, pool_params, i, j)
            pool_p.append(st)
            xs, pool_s = _pool_sample(xs, state_pool, pool_params, i, j, pool_s, seq=seq)
        else:
            final = i == DEPTH - 1
            xp, cst, sst = _ssd_prompt(xp, ssd_params, i, j, final=final)
            conv_p.append(cst)
            ssm_p.append(sst.reshape(-1, N_HEADS, HEAD_DIM, D_STATE))
            xs, *ssd_s = _ssd_sample(xs, state_conv, ssm_in, ssd_params, i, j, ssd_s, seq=seq, final=final)
    conv_s, ssm_s = ssd_s
    return (xp, xs.reshape(nb, seq, D_MODEL), jnp.stack(pool_p), pool_s, jnp.stack(conv_p), conv_s,
            jnp.stack(ssm_p), ssm_s.reshape(state_ssm.shape))
```

```python
import functools

import jax
import jax.numpy as jnp
from jax import lax
from jax.experimental import pallas as pl
from jax.experimental.pallas import tpu as pltpu

D_MODEL = 1024
DEPTH = 4
PAST_LEN = 16384
D_INNER = 2 * D_MODEL
POOL_WINDOWS = (2, 4, 8, 16)
POOL_GROUP = D_INNER // len(POOL_WINDOWS)
POOL_BUF = max(POOL_WINDOWS) - 1
HEAD_DIM = 64
N_HEADS = D_INNER // HEAD_DIM
D_STATE = 128
N_GROUPS = 4
HEADS_PER_GROUP = N_HEADS // N_GROUPS
GROUP_W = HEADS_PER_GROUP * HEAD_DIM
CONV_K = 4
CONV_DIM = D_INNER + 2 * N_GROUPS * D_STATE
CHUNK = 128
EPS = 1e-6

F32 = jnp.float32
BF16 = jnp.bfloat16

SUBLANES = 8
SLAB = 512
VMEM_LIMIT_BYTES = 56 * 1024 * 1024

N_WIN = 2 * D_INNER // SLAB
N_WOUT = D_MODEL // SLAB
N_WZ = D_INNER // SLAB
N_WXBC = CONV_DIM // SLAB

POOL_TL = 512
SSD_TL = 256
POOL_BT = 32
SSD_BT = 16
SSD_SB = 4


def _const_spec(shape):
    zeros = (0,) * len(shape)
    return pl.BlockSpec(shape, lambda *_: zeros, pipeline_mode=pl.Buffered(1))


def _layer_spec(shape, layer):
    idx = (layer,) + (0,) * len(shape)
    return pl.BlockSpec((None,) + tuple(shape), lambda *_: idx, pipeline_mode=pl.Buffered(1))


def _slab_specs(rows, layer, first, count):
    def spec(c):
        return pl.BlockSpec((None, rows, SLAB), lambda *_: (layer, 0, c), pipeline_mode=pl.Buffered(1))
    return [spec(first + c) for c in range(count)]


def _split(refs, *counts):
    out, pos = [], 0
    for n in counts:
        if n is None:
            out.append(refs[pos])
            pos += 1
        else:
            out.append(refs[pos:pos + n])
            pos += n
    assert pos == len(refs)
    return out


def _wcols(slabs, c0, width, krows=slice(None)):
    s0, off = divmod(c0, SLAB)
    if width <= SLAB:
        return slabs[s0][krows, off:off + width]
    return jnp.concatenate([slabs[s0 + i][krows, :] for i in range(width // SLAB)], axis=1)


def _rms(x, w):
    return x * lax.rsqrt(jnp.mean(x * x, axis=-1, keepdims=True) + EPS) * w


def _silu(x):
    half = 0.5 * x
    return half + half * jnp.tanh(half)


def _softplus(x):
    return jnp.maximum(x, 0.0) + jnp.log1p(jnp.exp(-jnp.abs(x)))


def _split3(v):
    v1 = v.astype(BF16).astype(F32)
    r1 = v - v1
    v2 = r1.astype(BF16).astype(F32)
    v3 = (r1 - v2).astype(BF16).astype(F32)
    return v1, v2, v3


def _sum3(f, v):
    p1, p2, p3 = (f(p.astype(BF16)) for p in _split3(v))
    return p1 + p2 + p3


def _dot01(m01, v):
    return _sum3(lambda p: jnp.dot(m01, p, preferred_element_type=F32), v)


def _dot01_tn(v, m01):
    dn = (((0,), (0,)), ((), ()))
    return _sum3(lambda p: lax.dot_general(p, m01, dn, preferred_element_type=F32), v)


def _dot01_r(v, m01):
    return _sum3(lambda p: jnp.dot(p, m01, preferred_element_type=F32), v)


def _expand_heads(v, r3):
    parts = jnp.concatenate(_split3(v), axis=1).astype(BF16)
    return jnp.dot(parts, r3, preferred_element_type=F32)


_POOL_WEIGHT_COUNTS = (None, N_WIN, None, None, N_WOUT)


def _pool_weight_operands(params, i, j):
    norm_w, win, wmix, scale, wout = params
    operands = [norm_w] + [win] * N_WIN + [wmix, scale] + [wout] * N_WOUT
    specs = ([_const_spec(norm_w.shape)] + _slab_specs(D_MODEL, j, 0, N_WIN)
             + [_layer_spec(wmix.shape[1:], j), _const_spec(scale.shape)] + _slab_specs(D_INNER, j, 0, N_WOUT))
    return operands, specs


def _reorder_rows(perm, v):
    return jnp.dot(perm, v, preferred_element_type=F32).astype(BF16)


def _pool_tail(p_of_group, z_of_group, wmix_ref, scale, wout, unperm=None):
    acc = None
    for g in range(len(POOL_WINDOWS)):
        cols = slice(g * POOL_GROUP, (g + 1) * POOL_GROUP)
        mixed = jnp.dot(p_of_group(g).astype(BF16), wmix_ref[g], preferred_element_type=F32)
        y = (mixed * scale[:, cols] * _silu(z_of_group(g))).astype(BF16)
        if unperm is not None:
            y = _reorder_rows(unperm, y)
        part = jnp.dot(y, _wcols(wout, 0, D_MODEL, cols), preferred_element_type=F32)
        acc = part if acc is None else acc + part
    return acc


def _pool_prompt_kernel(*refs, tl, nt, i, j):
    x_ref, nw_ref, win, wmix_ref, scale_ref, wout, xo_ref, st_ref, ext_ref = _split(
        refs, None, *_POOL_WEIGHT_COUNTS, None, None, None)
    t = pl.program_id(1)
    hist = 2 * SUBLANES
    x = x_ref[0]
    h = _rms(x, nw_ref[i:i + 1, :]).astype(BF16)

    @pl.when(t == 0)
    def _():
        ext_ref[0:hist, :] = jnp.zeros((hist, D_INNER), F32)

    ext_ref[hist:hist + tl, :] = jnp.dot(h, _wcols(win, 0, D_INNER), preferred_element_type=F32)
    pos = t * tl + lax.broadcasted_iota(jnp.int32, (tl, 1), 0)

    def p_of_group(g):
        w = POOL_WINDOWS[g]
        cols = slice(g * POOL_GROUP, (g + 1) * POOL_GROUP)
        u = ext_ref[hist:hist + tl, cols]
        s = u
        for k in range(1, w):
            s = s + ext_ref[hist - k:hist - k + tl, cols]
        cnt = jnp.minimum(pos + 1, w).astype(F32)
        return s * (1.0 / cnt) - u

    def z_of_group(g):
        return jnp.dot(h, _wcols(win, D_INNER + g * POOL_GROUP, POOL_GROUP), preferred_element_type=F32)

    xo_ref[0] = x + _pool_tail(p_of_group, z_of_group, wmix_ref, scale_ref[j:j + 1, :], wout)

    @pl.when(t == nt - 1)
    def _():
        st_ref[0] = ext_ref[hist + tl - POOL_BUF:hist + tl, :]

    ext_ref[0:hist, :] = ext_ref[tl:tl + hist, :]


def _pool_prompt(x, params, i, j):
    b, l, _ = x.shape
    tl = POOL_TL
    nt = l // tl
    w_operands, w_specs = _pool_weight_operands(params, i, j)
    return pl.pallas_call(
        functools.partial(_pool_prompt_kernel, tl=tl, nt=nt, i=i, j=j),
        grid=(b, nt),
        in_specs=[pl.BlockSpec((1, tl, D_MODEL), lambda s, t: (s, t, 0))] + w_specs,
        out_specs=[
            pl.BlockSpec((1, tl, D_MODEL), lambda s, t: (s, t, 0)),
            pl.BlockSpec((1, POOL_BUF, D_INNER), lambda s, t: (s, 0, 0)),
        ],
        out_shape=[
            jax.ShapeDtypeStruct((b, l, D_MODEL), F32),
            jax.ShapeDtypeStruct((b, POOL_BUF, D_INNER), F32),
        ],
        scratch_shapes=[pltpu.VMEM((tl + 2 * SUBLANES, D_INNER), F32)],
        compiler_params=pltpu.CompilerParams(
            dimension_semantics=("arbitrary", "arbitrary"), vmem_limit_bytes=VMEM_LIMIT_BYTES),
        name="pool_prompt",
    )(x, *w_operands)


def _pool_sample_kernel(*refs, bt, seq, start, i, j, n_prev):
    x_ref, buf_ref, perm_ref, unperm_ref, nw_ref, win, wmix_ref, scale_ref, wout, _, xo_ref, st_ref, u_ref = _split(
        refs, None, None, None, None, *_POOL_WEIGHT_COUNTS, n_prev, None, None, None)
    x = x_ref[...]
    h = _reorder_rows(perm_ref[...], _rms(x, nw_ref[i:i + 1, :]).astype(BF16))
    u_ref[...] = jnp.dot(h, _wcols(win, 0, D_INNER), preferred_element_type=F32).reshape(seq, bt, D_INNER)

    def src(t, cols):
        return u_ref[t, :, cols] if t >= 0 else buf_ref[0, POOL_BUF + t, :, cols]

    def p_of_group(g):
        w = POOL_WINDOWS[g]
        cols = slice(g * POOL_GROUP, (g + 1) * POOL_GROUP)
        outs = []
        for t in range(seq):
            cur = src(t, cols)
            s = cur
            for k in range(1, w):
                s = s + src(t - k, cols)
            outs.append(s * (1.0 / min(start + t + 1, w)) - cur)
        return jnp.concatenate(outs, axis=0)

    def z_of_group(g):
        return jnp.dot(h, _wcols(win, D_INNER + g * POOL_GROUP, POOL_GROUP), preferred_element_type=F32)

    xo_ref[...] = x + _pool_tail(p_of_group, z_of_group, wmix_ref, scale_ref[j:j + 1, :], wout, unperm_ref[...])
    keep = POOL_BUF - seq
    st_ref[0, 0:keep] = buf_ref[0, seq:POOL_BUF]
    st_ref[0, keep:POOL_BUF] = u_ref[...]


def _token_major_perm(bt, seq):
    r = jnp.arange(bt * seq)
    perm = (r[None, :] == ((r % bt) * seq + r // bt)[:, None]).astype(BF16)
    return perm, perm.T


def _pool_sample(x, state, params, i, j, prev, *, seq):
    rows = x.shape[0]
    nb = rows // seq
    bt = POOL_BT
    w_operands, w_specs = _pool_weight_operands(params, i, j)
    perms = _token_major_perm(bt, seq)
    operands = [x, state, *perms] + w_operands
    prev = [] if prev is None else [prev]
    aliases = {len(operands) + k: 1 + k for k in range(len(prev))}
    state_spec = pl.BlockSpec((1, POOL_BUF, bt, D_INNER), lambda s: (j, 0, s, 0))
    return pl.pallas_call(
        functools.partial(_pool_sample_kernel, bt=bt, seq=seq, start=PAST_LEN, i=i, j=j, n_prev=len(prev)),
        grid=(nb // bt,),
        in_specs=[pl.BlockSpec((bt * seq, D_MODEL), lambda s: (s, 0)), state_spec]
        + [_const_spec(p.shape) for p in perms] + w_specs + [pl.BlockSpec(memory_space=pl.ANY)] * len(prev),
        out_specs=[pl.BlockSpec((bt * seq, D_MODEL), lambda s: (s, 0)), state_spec],
        out_shape=[jax.ShapeDtypeStruct((rows, D_MODEL), F32), jax.ShapeDtypeStruct(state.shape, F32)],
        input_output_aliases=aliases,
        scratch_shapes=[pltpu.VMEM((seq, bt, D_INNER), F32)],
        compiler_params=pltpu.CompilerParams(
            dimension_semantics=("arbitrary",), vmem_limit_bytes=VMEM_LIMIT_BYTES),
        name="pool_sample",
    )(*operands, *prev)


_SSD_WEIGHT_COUNTS = (None, N_WZ, N_WXBC) + (None,) * 7 + (N_WOUT,) + (None,) * 6


def _ssd_block_masks(q):
    i = jnp.arange(CHUNK)[:, None]
    j = jnp.arange(CHUNK)[None, :]
    same = (i // q) == (j // q)
    tril = (same & (j <= i)).astype(BF16)
    ones = same.astype(BF16)
    eye = (i == j).astype(BF16)
    return [tril, tril.T, ones, eye]


def _head_expander():
    h = jnp.arange(N_HEADS)[:, None]
    c = jnp.arange(D_INNER)[None, :] // HEAD_DIM
    r = (h == c).astype(BF16)
    return jnp.concatenate([r, r, r], axis=0)


def _ssd_weight_operands(params, j, q):
    norm_w, w_in, w_dt, conv_w, conv_b, dt_bias, a_log, d_cols, ssd_norm_w, w_out, final_norm_w = params
    consts = _ssd_block_masks(q) + [_head_expander()]
    small = [conv_w, conv_b, dt_bias, a_log, d_cols, ssd_norm_w]
    operands = ([norm_w] + [w_in] * (N_WZ + N_WXBC) + [w_dt] + small + [w_out] * N_WOUT + [final_norm_w] + consts)
    specs = ([_const_spec(norm_w.shape)] + _slab_specs(D_MODEL, j, 0, N_WZ) + _slab_specs(D_MODEL, j, N_WZ, N_WXBC)
             + [_layer_spec(w_dt.shape[1:], j)] + [_const_spec(a.shape) for a in small]
             + _slab_specs(D_INNER, j, 0, N_WOUT)
             + [_const_spec(final_norm_w.shape)] + [_const_spec(c.shape) for c in consts])
    return operands, specs


def _dt_and_decay(dt_raw, dtb_ref, alog_ref, j):
    dt = _softplus(dt_raw + dtb_ref[j:j + 1, :])
    return dt, dt * (-jnp.exp(alog_ref[j:j + 1, :]))


def _ssd_block_diag(r0, a_c, dt_c, act_ref, ybuf_ref, tril, triu, ones, eye, q, side_work=None):
    acum = _dot01(tril, a_c)
    acum_t = _dot01_tn(a_c, triu)
    alast = _dot01(ones, a_c)
    dt_t = _dot01_tn(dt_c, eye)
    ii = lax.broadcasted_iota(jnp.int32, (CHUNK, CHUNK), 0)
    jj = lax.broadcasted_iota(jnp.int32, (CHUNK, CHUNK), 1)
    causal = jj <= ii
    if q != CHUNK:
        causal = causal & ((ii // q) == (jj // q))
    rows = slice(r0, r0 + CHUNK)
    first_of_pair = lax.broadcasted_iota(jnp.int32, (CHUNK, 2 * HEAD_DIM), 1) < HEAD_DIM
    for g in range(N_GROUPS):
        if side_work is not None:
            side_work(g)
        b0 = D_INNER + g * D_STATE
        c0 = D_INNER + N_GROUPS * D_STATE + g * D_STATE
        bg = act_ref[rows, b0:b0 + D_STATE].astype(BF16)
        cg = act_ref[rows, c0:c0 + D_STATE].astype(BF16)
        cb = lax.dot_general(cg, bg, (((1,), (1,)), ((), ())), preferred_element_type=F32)
        for pair in range(HEADS_PER_GROUP // 2):
            h0 = g * HEADS_PER_GROUP + 2 * pair
            ws = []
            for hh in (h0, h0 + 1):
                seg = acum[:, hh:hh + 1] - acum_t[hh:hh + 1, :]
                decay = jnp.exp(jnp.where(causal, seg, -jnp.inf))
                ws.append((cb * decay * dt_t[hh:hh + 1, :]).astype(BF16))
            pcols = slice(h0 * HEAD_DIM, (h0 + 2) * HEAD_DIM)
            xp = act_ref[rows, pcols]
            rhs = jnp.concatenate([jnp.where(first_of_pair, xp, 0.0), jnp.where(first_of_pair, 0.0, xp)], axis=0)
            ybuf_ref[rows, pcols] = jnp.dot(jnp.concatenate(ws, axis=1), rhs.astype(BF16),
                                            preferred_element_type=F32)
    return acum, alast


def _ssd_epilogue(x, gate, act_ref, ybuf_ref, d_cols, norm_w, wout, fnw_ref, final):
    y = ybuf_ref[...] + d_cols * act_ref[:, :D_INNER]
    y = _rms(y * gate, norm_w)
    out = x + jnp.dot(y.astype(BF16), _wcols(wout, 0, D_MODEL), preferred_element_type=F32)
    return _rms(out, fnw_ref[...]) if final else out


def _causal_conv(ext, cw_ref, cb_ref, j):
    out = ext(0) * cw_ref[j, 0:1, :]
    for k in range(1, CONV_K):
        out = out + ext(k) * cw_ref[j, k:k + 1, :]
    return _silu(out + cb_ref[j:j + 1, :])


def _ssd_prompt_kernel(*refs, tl, nt, i, j, final):
    (x_ref, nw_ref, wz, wxbc, wdt_ref, cw_ref, cb_ref, dtb_ref, alog_ref, dx_ref, normw_ref, wout, fnw_ref,
     tril_ref, triu_ref, ones_ref, eye_ref, r3_ref, xo_ref, cst_ref, ssm_ref,
     cext_ref, act_ref, ybuf_ref, ht_ref, gate_ref) = _split(refs, None, *_SSD_WEIGHT_COUNTS, *(None,) * 8)
    t = pl.program_id(1)

    @pl.when(t == 0)
    def _():
        cext_ref[0:SUBLANES, :] = jnp.zeros((SUBLANES, CONV_DIM), F32)
        ht_ref[...] = jnp.zeros_like(ht_ref)

    h = _rms(x_ref[0], nw_ref[i:i + 1, :]).astype(BF16)
    cext_ref[SUBLANES:SUBLANES + tl, :] = jnp.dot(h, _wcols(wxbc, 0, CONV_DIM), preferred_element_type=F32)
    dt, a = _dt_and_decay(jnp.dot(h, wdt_ref[...], preferred_element_type=F32), dtb_ref, alog_ref, j)
    lo = SUBLANES - (CONV_K - 1)
    act_ref[...] = _causal_conv(lambda k: cext_ref[lo + k:lo + k + tl, :], cw_ref, cb_ref, j)

    tril, triu, ones, eye = tril_ref[...], triu_ref[...], ones_ref[...], eye_ref[...]
    n_chunks = tl // CHUNK
    zw = D_INNER // (n_chunks * N_GROUPS)
    for c in range(n_chunks):
        r0 = c * CHUNK
        rows = slice(r0, r0 + CHUNK)
        a_c, dt_c = a[rows], dt[rows]

        def gate_piece(g, c=c):
            z0 = (c * N_GROUPS + g) * zw
            gate_ref[:, z0:z0 + zw] = _silu(jnp.dot(h, _wcols(wz, z0, zw), preferred_element_type=F32))

        acum, alast = _ssd_block_diag(r0, a_c, dt_c, act_ref, ybuf_ref, tril, triu, ones, eye, CHUNK, gate_piece)
        scales = jnp.concatenate(
            [jnp.exp(acum), jnp.exp(alast - acum) * dt_c, jnp.exp(alast[0:SUBLANES])], axis=0)
        scales_x = _expand_heads(scales, r3_ref[...])
        ea_x = scales_x[0:CHUNK]
        dte_x = scales_x[CHUNK:2 * CHUNK]
        cd_x = scales_x[2 * CHUNK:2 * CHUNK + 1]
        for g in range(N_GROUPS):
            gcols = slice(g * GROUP_W, (g + 1) * GROUP_W)
            b0 = D_INNER + g * D_STATE
            c0 = D_INNER + N_GROUPS * D_STATE + g * D_STATE
            bg = act_ref[rows, b0:b0 + D_STATE].astype(BF16)
            cg = act_ref[rows, c0:c0 + D_STATE].astype(BF16)
            hprev = ht_ref[:, gcols]
            y_off = jnp.dot(cg, hprev.astype(BF16), preferred_element_type=F32) * ea_x[:, gcols]
            ybuf_ref[rows, gcols] = ybuf_ref[rows, gcols] + y_off
            xs = (act_ref[rows, gcols] * dte_x[:, gcols]).astype(BF16)
            st = lax.dot_general(bg, xs, (((0,), (0,)), ((), ())), preferred_element_type=F32)
            ht_ref[:, gcols] = hprev * cd_x[:, gcols] + st

    xo_ref[0] = _ssd_epilogue(x_ref[0], gate_ref[...], act_ref, ybuf_ref, dx_ref[j:j + 1, :], normw_ref[j:j + 1, :],
                              wout, fnw_ref, final)

    @pl.when(t == nt - 1)
    def _():
        cst_ref[0] = cext_ref[SUBLANES + tl - (CONV_K - 1):SUBLANES + tl, :]
        ssm_ref[0] = ht_ref[...].T

    cext_ref[0:SUBLANES, :] = cext_ref[tl:tl + SUBLANES, :]


def _ssd_prompt(x, params, i, j, *, final):
    b, l, _ = x.shape
    tl = SSD_TL
    nt = l // tl
    w_operands, w_specs = _ssd_weight_operands(params, j, CHUNK)
    return pl.pallas_call(
        functools.partial(_ssd_prompt_kernel, tl=tl, nt=nt, i=i, j=j, final=final),
        grid=(b, nt),
        in_specs=[pl.BlockSpec((1, tl, D_MODEL), lambda s, t: (s, t, 0))] + w_specs,
        out_specs=[
            pl.BlockSpec((1, tl, D_MODEL), lambda s, t: (s, t, 0)),
            pl.BlockSpec((1, CONV_K - 1, CONV_DIM), lambda s, t: (s, 0, 0)),
            pl.BlockSpec((1, N_HEADS * HEAD_DIM, D_STATE), lambda s, t: (s, 0, 0)),
        ],
        out_shape=[
            jax.ShapeDtypeStruct((b, l, D_MODEL), F32),
            jax.ShapeDtypeStruct((b, CONV_K - 1, CONV_DIM), F32),
            jax.ShapeDtypeStruct((b, N_HEADS * HEAD_DIM, D_STATE), F32),
        ],
        scratch_shapes=[
            pltpu.VMEM((tl + SUBLANES, CONV_DIM), F32),
            pltpu.VMEM((tl, CONV_DIM), F32),
            pltpu.VMEM((tl, D_INNER), F32),
            pltpu.VMEM((D_STATE, N_HEADS * HEAD_DIM), F32),
            pltpu.VMEM((tl, D_INNER), F32),
        ],
        compiler_params=pltpu.CompilerParams(
            dimension_semantics=("arbitrary", "arbitrary"), vmem_limit_bytes=VMEM_LIMIT_BYTES),
        name="ssd_prompt",
    )(x, *w_operands)


def _ssd_sample_kernel(*refs, bt, sb, seq, i, j, final, n_prev):
    (x_ref, cin_ref, hin_ref, nw_ref, wz, wxbc, wdt_ref, cw_ref, cb_ref, dtb_ref, alog_ref, dx_ref, normw_ref, wout,
     fnw_ref, tril_ref, triu_ref, ones_ref, eye_ref, r3_ref, _, xo_ref, cst_ref, hout_ref,
     cext_ref, act_ref, ybuf_ref, eax_ref, xsc_ref, cdt_ref, h_ref) = _split(
         refs, None, None, None, *_SSD_WEIGHT_COUNTS, n_prev, *(None,) * 10)
    s_idx = pl.program_id(1)
    n_inner = bt // sb
    rows_total = bt * seq

    @pl.when(s_idx == 0)
    def _():
        h = _rms(x_ref[...], nw_ref[i:i + 1, :]).astype(BF16)
        h_ref[...] = h
        xbc = jnp.dot(h, _wcols(wxbc, 0, CONV_DIM), preferred_element_type=F32)
        dt, a = _dt_and_decay(jnp.dot(h, wdt_ref[...], preferred_element_type=F32), dtb_ref, alog_ref, j)
        lo = SUBLANES - (CONV_K - 1)
        cext_ref[:, 0:SUBLANES, :] = jnp.zeros((bt, SUBLANES, CONV_DIM), F32)
        cext_ref[:, lo:SUBLANES, :] = cin_ref[0]
        cext_ref[:, SUBLANES:SUBLANES + seq, :] = xbc.reshape(bt, seq, CONV_DIM)
        conv = _causal_conv(lambda k: cext_ref[:, lo + k:lo + k + seq, :], cw_ref, cb_ref, j)
        act_ref[...] = conv.reshape(rows_total, CONV_DIM)
        cst_ref[0] = cext_ref[:, SUBLANES + seq - (CONV_K - 1):SUBLANES + seq, :]
        tril, triu, ones, eye = tril_ref[...], triu_ref[...], ones_ref[...], eye_ref[...]
        for c in range(rows_total // CHUNK):
            r0 = c * CHUNK
            rows = slice(r0, r0 + CHUNK)
            a_c, dt_c = a[rows], dt[rows]
            acum, alast = _ssd_block_diag(r0, a_c, dt_c, act_ref, ybuf_ref, tril, triu, ones, eye, seq)
            scales = jnp.concatenate([jnp.exp(acum), jnp.exp(alast - acum) * dt_c], axis=0)
            scales_x = _expand_heads(scales, r3_ref[...])
            eax_ref[rows, :] = scales_x[0:CHUNK]
            xsc_ref[rows, :] = act_ref[rows, 0:D_INNER] * scales_x[CHUNK:2 * CHUNK]
            cdt_ref[:, rows] = jnp.exp(_dot01_tn(a_c, ones))

    for si in range(sb):
        r0 = pl.multiple_of((s_idx * sb + si) * seq, seq)
        rows = pl.ds(r0, seq)
        pick = (lax.broadcasted_iota(jnp.int32, (rows_total, D_STATE), 0) == r0).astype(BF16)
        cd = _dot01_r(cdt_ref[...], pick)
        for g in range(N_GROUPS):
            gcols = slice(g * GROUP_W, (g + 1) * GROUP_W)
            b0 = D_INNER + g * D_STATE
            c0 = D_INNER + N_GROUPS * D_STATE + g * D_STATE
            bg = act_ref[rows, b0:b0 + D_STATE].astype(BF16)
            cg = act_ref[rows, c0:c0 + D_STATE].astype(BF16)
            hprev = hin_ref[0, si, g * GROUP_W:(g + 1) * GROUP_W, :]
            y_off = lax.dot_general(cg, hprev.astype(BF16), (((1,), (1,)), ((), ())), preferred_element_type=F32)
            ybuf_ref[rows, gcols] = ybuf_ref[rows, gcols] + y_off * eax_ref[rows, gcols]
            st = lax.dot_general(xsc_ref[rows, gcols].astype(BF16), bg, (((0,), (0,)), ((), ())),
                                 preferred_element_type=F32)
            for e in range(HEADS_PER_GROUP):
                hh = g * HEADS_PER_GROUP + e
                hrows = slice(hh * HEAD_DIM, (hh + 1) * HEAD_DIM)
                hout_ref[0, si, hrows, :] = (hin_ref[0, si, hrows, :] * cd[hh:hh + 1, :]
                                             + st[e * HEAD_DIM:(e + 1) * HEAD_DIM, :])

    @pl.when(s_idx == n_inner - 1)
    def _():
        gate = _silu(jnp.dot(h_ref[...], _wcols(wz, 0, D_INNER), preferred_element_type=F32))
        xo_ref[...] = _ssd_epilogue(x_ref[...], gate, act_ref, ybuf_ref, dx_ref[j:j + 1, :], normw_ref[j:j + 1, :],
                                    wout, fnw_ref, final)


def _ssd_sample(x, state_conv, state_ssm, params, i, j, prev, *, seq, final):
    rows = x.shape[0]
    nb = rows // seq
    bt, sb = SSD_BT, SSD_SB
    n_inner = bt // sb
    hp = N_HEADS * HEAD_DIM
    w_operands, w_specs = _ssd_weight_operands(params, j, seq)
    operands = [x, state_conv, state_ssm] + w_operands
    prev = [] if prev is None else list(prev)
    aliases = {len(operands) + k: 1 + k for k in range(len(prev))}
    n_layers = state_ssm.shape[0]
    return pl.pallas_call(
        functools.partial(_ssd_sample_kernel, bt=bt, sb=sb, seq=seq, i=i, j=j, final=final, n_prev=len(prev)),
        grid=(nb // bt, n_inner),
        in_specs=[
            pl.BlockSpec((bt * seq, D_MODEL), lambda b, s: (b, 0)),
            pl.BlockSpec((1, bt, CONV_K - 1, CONV_DIM), lambda b, s: (j, b, 0, 0)),
            pl.BlockSpec((1, sb, hp, D_STATE), lambda b, s: (j, b * n_inner + s, 0, 0)),
        ] + w_specs + [pl.BlockSpec(memory_space=pl.ANY)] * len(prev),
        out_specs=[
            pl.BlockSpec((bt * seq, D_MODEL), lambda b, s: (b, 0)),
            pl.BlockSpec((1, bt, CONV_K - 1, CONV_DIM), lambda b, s: (j, b, 0, 0)),
            pl.BlockSpec((1, sb, hp, D_STATE), lambda b, s: (j, b * n_inner + s, 0, 0)),
        ],
        out_shape=[
            jax.ShapeDtypeStruct((rows, D_MODEL), F32),
            jax.ShapeDtypeStruct((n_layers, nb, CONV_K - 1, CONV_DIM), F32),
            jax.ShapeDtypeStruct((n_layers, nb, hp, D_STATE), F32),
        ],
        input_output_aliases=aliases,
        scratch_shapes=[
            pltpu.VMEM((bt, SUBLANES + seq, CONV_DIM), F32),
            pltpu.VMEM((bt * seq, CONV_DIM), F32),
            pltpu.VMEM((bt * seq, D_INNER), F32),
            pltpu.VMEM((bt * seq, D_INNER), F32),
            pltpu.VMEM((bt * seq, D_INNER), F32),
            pltpu.VMEM((N_HEADS, bt * seq), F32),
            pltpu.VMEM((bt * seq, D_MODEL), BF16),
        ],
        compiler_params=pltpu.CompilerParams(
            dimension_semantics=("arbitrary", "arbitrary"), vmem_limit_bytes=VMEM_LIMIT_BYTES),
        name="ssd_sample",
    )(*operands, *prev)


def kernel(x_prompt, x_sample, state_pool, state_conv, state_ssm, norm_w, pool_in_w, pool_mix_w, pool_scale,
           pool_out_w, ssd_in_w, ssd_conv_w, ssd_conv_b, ssd_dt_bias, ssd_A_log, ssd_D, ssd_norm_w, ssd_out_w,
           final_norm_w):
    nb, seq, _ = x_sample.shape
    hp = N_HEADS * HEAD_DIM
    xp = x_prompt
    xs = x_sample.reshape(nb * seq, D_MODEL)
    ssm_in = state_ssm.reshape(state_ssm.shape[0], nb, hp, D_STATE)
    pool_in = state_pool.transpose(0, 2, 1, 3)
    pool_params = (norm_w, pool_in_w.astype(BF16), pool_mix_w.astype(BF16), pool_scale, pool_out_w.astype(BF16))
    n_main = D_INNER + CONV_DIM
    ssd_params = (norm_w, ssd_in_w[:, :, :n_main].astype(BF16), ssd_in_w[:, :, n_main:].astype(BF16),
                  ssd_conv_w, ssd_conv_b, ssd_dt_bias, ssd_A_log,
                  jnp.repeat(ssd_D, HEAD_DIM, axis=1), ssd_norm_w, ssd_out_w.astype(BF16),
                  final_norm_w.reshape(1, D_MODEL))
    pool_p, conv_p, ssm_p = [], [], []
    pool_s = ssd_s = None
    for i in range(DEPTH):
        j = i // 2
        if i % 2 == 0:
            xp, st = _pool_prompt(xp, pool_params, i, j)
            pool_p.append(st)
            xs, pool_s = _pool_sample(xs, pool_in, pool_params, i, j, pool_s, seq=seq)
        else:
            final = i == DEPTH - 1
            xp, cst, sst = _ssd_prompt(xp, ssd_params, i, j, final=final)
            conv_p.append(cst)
            ssm_p.append(sst.reshape(-1, N_HEADS, HEAD_DIM, D_STATE))
            xs, *ssd_s = _ssd_sample(xs, state_conv, ssm_in, ssd_params, i, j, ssd_s, seq=seq, final=final)
    conv_s, ssm_s = ssd_s
    return (xp, xs.reshape(nb, seq, D_MODEL), jnp.stack(pool_p), pool_s.transpose(0, 2, 1, 3), jnp.stack(conv_p), conv_s,
            jnp.stack(ssm_p), ssm_s.reshape(state_ssm.shape))
```

```python
import functools

import jax
import jax.numpy as jnp
from jax import lax
from jax.experimental import pallas as pl
from jax.experimental.pallas import tpu as pltpu

D_MODEL = 1024
DEPTH = 4
PAST_LEN = 16384
D_INNER = 2 * D_MODEL
POOL_WINDOWS = (2, 4, 8, 16)
POOL_GROUP = D_INNER // len(POOL_WINDOWS)
POOL_BUF = max(POOL_WINDOWS) - 1
HEAD_DIM = 64
N_HEADS = D_INNER // HEAD_DIM
D_STATE = 128
N_GROUPS = 4
HEADS_PER_GROUP = N_HEADS // N_GROUPS
GROUP_W = HEADS_PER_GROUP * HEAD_DIM
CONV_K = 4
CONV_DIM = D_INNER + 2 * N_GROUPS * D_STATE
CHUNK = 128
EPS = 1e-6

F32 = jnp.float32
BF16 = jnp.bfloat16

SUBLANES = 8
LANES = 128
SLAB = 512
VMEM_LIMIT_BYTES = 60 * 1024 * 1024

N_WIN = 2 * D_INNER // SLAB
N_WOUT = D_MODEL // SLAB
N_WZ = D_INNER // SLAB
N_WXBC = CONV_DIM // SLAB
DT_LANE_BLOCK = (D_INNER + CONV_DIM) // LANES

POOL_TL = 1024
SSD_TL = 256
POOL_BT = 32


def _const_spec(shape):
    zeros = (0,) * len(shape)
    return pl.BlockSpec(shape, lambda *_: zeros, pipeline_mode=pl.Buffered(1))


def _layer_spec(shape, layer):
    idx = (layer,) + (0,) * len(shape)
    return pl.BlockSpec((None,) + tuple(shape), lambda *_: idx, pipeline_mode=pl.Buffered(1))


def _slab_specs(rows, layer, first, count):
    def spec(c):
        return pl.BlockSpec((None, rows, SLAB), lambda *_: (layer, 0, c), pipeline_mode=pl.Buffered(1))
    return [spec(first + c) for c in range(count)]


def _split(refs, *counts):
    out, pos = [], 0
    for n in counts:
        if n is None:
            out.append(refs[pos])
            pos += 1
        else:
            out.append(refs[pos:pos + n])
            pos += n
    assert pos == len(refs)
    return out


def _wcols(slabs, c0, width, krows=slice(None)):
    s0, off = divmod(c0, SLAB)
    if width <= SLAB:
        return slabs[s0][krows, off:off + width]
    return jnp.concatenate([slabs[s0 + i][krows, :] for i in range(width // SLAB)], axis=1)


def _rms(x, w):
    return x * lax.rsqrt(jnp.mean(x * x, axis=-1, keepdims=True) + EPS) * w


def _silu(x):
    half = 0.5 * x
    return half + half * jnp.tanh(half)


def _softplus(x):
    return jnp.maximum(x, 0.0) + jnp.log1p(jnp.exp(-jnp.abs(x)))


def _split3(v):
    v1 = v.astype(BF16).astype(F32)
    r1 = v - v1
    v2 = r1.astype(BF16).astype(F32)
    v3 = (r1 - v2).astype(BF16).astype(F32)
    return v1, v2, v3


def _sum3(f, v):
    p1, p2, p3 = (f(p.astype(BF16)) for p in _split3(v))
    return p1 + p2 + p3


def _dot01(m01, v):
    return _sum3(lambda p: jnp.dot(m01, p, preferred_element_type=F32), v)


def _dot01_tn(v, m01):
    dn = (((0,), (0,)), ((), ()))
    return _sum3(lambda p: lax.dot_general(p, m01, dn, preferred_element_type=F32), v)


def _dot01_r(v, m01):
    return _sum3(lambda p: jnp.dot(p, m01, preferred_element_type=F32), v)


def _expand_heads(v, r3):
    parts = jnp.concatenate(_split3(v), axis=1).astype(BF16)
    return jnp.dot(parts, r3, preferred_element_type=F32)


_POOL_WEIGHT_COUNTS = (None, N_WIN, None, None, N_WOUT)


def _pool_weight_operands(params, i, j):
    norm_w, win, wmix, scale, wout = params
    operands = [norm_w] + [win] * N_WIN + [wmix, scale] + [wout] * N_WOUT
    specs = ([_const_spec(norm_w.shape)] + _slab_specs(D_MODEL, j, 0, N_WIN)
             + [_layer_spec(wmix.shape[1:], j), _const_spec(scale.shape)] + _slab_specs(D_INNER, j, 0, N_WOUT))
    return operands, specs


def _reorder_rows(perm, v):
    return jnp.dot(perm, v, preferred_element_type=F32).astype(BF16)


def _pool_tail(p_of_group, z_of_group, wmix_ref, scale, wout, unperm=None):
    acc = None
    for g in range(len(POOL_WINDOWS)):
        cols = slice(g * POOL_GROUP, (g + 1) * POOL_GROUP)
        mixed = jnp.dot(p_of_group(g).astype(BF16), wmix_ref[g], preferred_element_type=F32)
        y = (mixed * scale[:, cols] * _silu(z_of_group(g))).astype(BF16)
        if unperm is not None:
            y = _reorder_rows(unperm, y)
        part = jnp.dot(y, _wcols(wout, 0, D_MODEL, cols), preferred_element_type=F32)
        acc = part if acc is None else acc + part
    return acc


def _pool_prompt_kernel(*refs, tl, nt, i, j):
    x_ref, nw_ref, win, wmix_ref, scale_ref, wout, xo_ref, st_ref, ext_ref = _split(
        refs, None, *_POOL_WEIGHT_COUNTS, None, None, None)
    t = pl.program_id(1)
    hist = 2 * SUBLANES
    x = x_ref[0]
    h = _rms(x, nw_ref[i:i + 1, :]).astype(BF16)

    @pl.when(t == 0)
    def _():
        ext_ref[0:hist, :] = jnp.zeros((hist, D_INNER), F32)

    ext_ref[hist:hist + tl, :] = jnp.dot(h, _wcols(win, 0, D_INNER), preferred_element_type=F32)
    pos = t * tl + lax.broadcasted_iota(jnp.int32, (tl, 1), 0)

    def p_of_group(g):
        w = POOL_WINDOWS[g]
        cols = slice(g * POOL_GROUP, (g + 1) * POOL_GROUP)
        u = ext_ref[hist:hist + tl, cols]
        s = u
        for k in range(1, w):
            s = s + ext_ref[hist - k:hist - k + tl, cols]
        cnt = jnp.minimum(pos + 1, w).astype(F32)
        return s * (1.0 / cnt) - u

    def z_of_group(g):
        return jnp.dot(h, _wcols(win, D_INNER + g * POOL_GROUP, POOL_GROUP), preferred_element_type=F32)

    xo_ref[0] = x + _pool_tail(p_of_group, z_of_group, wmix_ref, scale_ref[j:j + 1, :], wout)

    @pl.when(t == nt - 1)
    def _():
        st_ref[0] = ext_ref[hist + tl - POOL_BUF:hist + tl, :]

    ext_ref[0:hist, :] = ext_ref[tl:tl + hist, :]


def _pool_prompt(x, params, i, j):
    b, l, _ = x.shape
    tl = POOL_TL
    nt = l // tl
    w_operands, w_specs = _pool_weight_operands(params, i, j)
    return pl.pallas_call(
        functools.partial(_pool_prompt_kernel, tl=tl, nt=nt, i=i, j=j),
        grid=(b, nt),
        in_specs=[pl.BlockSpec((1, tl, D_MODEL), lambda s, t: (s, t, 0))] + w_specs,
        out_specs=[
            pl.BlockSpec((1, tl, D_MODEL), lambda s, t: (s, t, 0)),
            pl.BlockSpec((1, POOL_BUF, D_INNER), lambda s, t: (s, 0, 0)),
        ],
        out_shape=[
            jax.ShapeDtypeStruct((b, l, D_MODEL), F32),
            jax.ShapeDtypeStruct((b, POOL_BUF, D_INNER), F32),
        ],
        scratch_shapes=[pltpu.VMEM((tl + 2 * SUBLANES, D_INNER), F32)],
        compiler_params=pltpu.CompilerParams(
            dimension_semantics=("arbitrary", "arbitrary"), vmem_limit_bytes=VMEM_LIMIT_BYTES),
        name="pool_prompt",
    )(x, *w_operands)


def _pool_sample_kernel(*refs, bt, seq, start, i, j, n_prev):
    x_ref, buf_ref, perm_ref, unperm_ref, nw_ref, win, wmix_ref, scale_ref, wout, _, xo_ref, st_ref, u_ref = _split(
        refs, None, None, None, None, *_POOL_WEIGHT_COUNTS, n_prev, None, None, None)
    x = x_ref[...]
    h = _reorder_rows(perm_ref[...], _rms(x, nw_ref[i:i + 1, :]).astype(BF16))
    u_ref[...] = jnp.dot(h, _wcols(win, 0, D_INNER), preferred_element_type=F32).reshape(seq, bt, D_INNER)

    def src(t, cols):
        return u_ref[t, :, cols] if t >= 0 else buf_ref[0, POOL_BUF + t, :, cols]

    def p_of_group(g):
        w = POOL_WINDOWS[g]
        cols = slice(g * POOL_GROUP, (g + 1) * POOL_GROUP)
        outs = []
        for t in range(seq):
            cur = src(t, cols)
            s = cur
            for k in range(1, w):
                s = s + src(t - k, cols)
            outs.append(s * (1.0 / min(start + t + 1, w)) - cur)
        return jnp.concatenate(outs, axis=0)

    def z_of_group(g):
        return jnp.dot(h, _wcols(win, D_INNER + g * POOL_GROUP, POOL_GROUP), preferred_element_type=F32)

    xo_ref[...] = x + _pool_tail(p_of_group, z_of_group, wmix_ref, scale_ref[j:j + 1, :], wout, unperm_ref[...])
    keep = POOL_BUF - seq
    st_ref[0, 0:keep] = buf_ref[0, seq:POOL_BUF]
    st_ref[0, keep:POOL_BUF] = u_ref[...]


def _token_major_perm(bt, seq):
    r = jnp.arange(bt * seq)
    perm = (r[None, :] == ((r % bt) * seq + r // bt)[:, None]).astype(BF16)
    return perm, perm.T


def _pool_sample(x, state, params, i, j, prev, *, seq):
    rows = x.shape[0]
    nb = rows // seq
    bt = POOL_BT
    w_operands, w_specs = _pool_weight_operands(params, i, j)
    perms = _token_major_perm(bt, seq)
    operands = [x, state, *perms] + w_operands
    prev = [] if prev is None else [prev]
    aliases = {len(operands) + k: 1 + k for k in range(len(prev))}
    state_spec = pl.BlockSpec((1, POOL_BUF, bt, D_INNER), lambda s: (j, 0, s, 0))
    return pl.pallas_call(
        functools.partial(_pool_sample_kernel, bt=bt, seq=seq, start=PAST_LEN, i=i, j=j, n_prev=len(prev)),
        grid=(nb // bt,),
        in_specs=[pl.BlockSpec((bt * seq, D_MODEL), lambda s: (s, 0)), state_spec]
        + [_const_spec(p.shape) for p in perms] + w_specs + [pl.BlockSpec(memory_space=pl.ANY)] * len(prev),
        out_specs=[pl.BlockSpec((bt * seq, D_MODEL), lambda s: (s, 0)), state_spec],
        out_shape=[jax.ShapeDtypeStruct((rows, D_MODEL), F32), jax.ShapeDtypeStruct(state.shape, F32)],
        input_output_aliases=aliases,
        scratch_shapes=[pltpu.VMEM((seq, bt, D_INNER), F32)],
        compiler_params=pltpu.CompilerParams(
            dimension_semantics=("arbitrary",), vmem_limit_bytes=VMEM_LIMIT_BYTES),
        name="pool_sample",
    )(*operands, *prev)


_SSD_PARAM_COUNTS = (None, N_WZ, N_WXBC) + (None,) * 7 + (N_WOUT,) + (None,) * 2
N_MASKS = 4


def _ssd_block_masks(q):
    i = jnp.arange(CHUNK)[:, None]
    j = jnp.arange(CHUNK)[None, :]
    same = (i // q) == (j // q)
    tril = (same & (j <= i)).astype(BF16)
    ones = same.astype(BF16)
    eye = (i == j).astype(BF16)
    return [tril, tril.T, ones, eye]


def _head_expander():
    h = jnp.arange(N_HEADS)[:, None]
    c = jnp.arange(D_INNER)[None, :] // HEAD_DIM
    r = (h == c).astype(BF16)
    return jnp.concatenate([r, r, r], axis=0)


def _ssd_param_operands(params, j):
    norm_w, w_in, conv_w, conv_b, dt_bias, a_log, d_cols, ssd_norm_w, w_out, final_norm_w = params
    small = [conv_w, conv_b, dt_bias, a_log, d_cols, ssd_norm_w]
    tail = [final_norm_w, _head_expander()]
    operands = [norm_w] + [w_in] * (N_WZ + N_WXBC + 1) + small + [w_out] * N_WOUT + tail
    dt_spec = pl.BlockSpec((None, D_MODEL, LANES), lambda *_: (j, 0, DT_LANE_BLOCK), pipeline_mode=pl.Buffered(1))
    specs = ([_const_spec(norm_w.shape)] + _slab_specs(D_MODEL, j, 0, N_WZ) + _slab_specs(D_MODEL, j, N_WZ, N_WXBC)
             + [dt_spec] + [_const_spec(a.shape) for a in small] + _slab_specs(D_INNER, j, 0, N_WOUT)
             + [_const_spec(a.shape) for a in tail])
    return operands, specs


def _dt_and_decay(dt_raw, dtb_ref, alog_ref, j):
    dt = _softplus(dt_raw + dtb_ref[j:j + 1, :])
    return dt, dt * (-jnp.exp(alog_ref[j:j + 1, :]))


def _ssd_block_diag(r0, a_c, dt_c, act_ref, ybuf_ref, tril, triu, ones, eye, q, side_work=None):
    acum = _dot01(tril, a_c)
    acum_t = _dot01_tn(a_c, triu)
    alast = _dot01(ones, a_c)
    dt_t = _dot01_tn(dt_c, eye)
    ii = lax.broadcasted_iota(jnp.int32, (CHUNK, CHUNK), 0)
    jj = lax.broadcasted_iota(jnp.int32, (CHUNK, CHUNK), 1)
    causal = jj <= ii
    if q != CHUNK:
        causal = causal & ((ii // q) == (jj // q))
    rows = slice(r0, r0 + CHUNK)
    first_of_pair = lax.broadcasted_iota(jnp.int32, (CHUNK, 2 * HEAD_DIM), 1) < HEAD_DIM
    for g in range(N_GROUPS):
        if side_work is not None:
            side_work(g)
        b0 = D_INNER + g * D_STATE
        c0 = D_INNER + N_GROUPS * D_STATE + g * D_STATE
        bg = act_ref[rows, b0:b0 + D_STATE].astype(BF16)
        cg = act_ref[rows, c0:c0 + D_STATE].astype(BF16)
        cb = lax.dot_general(cg, bg, (((1,), (1,)), ((), ())), preferred_element_type=F32)
        for pair in range(HEADS_PER_GROUP // 2):
            h0 = g * HEADS_PER_GROUP + 2 * pair
            ws = []
            for hh in (h0, h0 + 1):
                seg = acum[:, hh:hh + 1] - acum_t[hh:hh + 1, :]
                decay = jnp.exp(jnp.where(causal, seg, -jnp.inf))
                ws.append((cb * decay * dt_t[hh:hh + 1, :]).astype(BF16))
            pcols = slice(h0 * HEAD_DIM, (h0 + 2) * HEAD_DIM)
            xp = act_ref[rows, pcols]
            rhs = jnp.concatenate([jnp.where(first_of_pair, xp, 0.0), jnp.where(first_of_pair, 0.0, xp)], axis=0)
            ybuf_ref[rows, pcols] = jnp.dot(jnp.concatenate(ws, axis=1), rhs.astype(BF16),
                                            preferred_element_type=F32)
    return acum, alast


def _ssd_epilogue(x, gate, act_ref, ybuf_ref, d_cols, norm_w, wout, fnw_ref, final):
    y = ybuf_ref[...] + d_cols * act_ref[:, :D_INNER]
    y = _rms(y * gate, norm_w)
    out = x + jnp.dot(y.astype(BF16), _wcols(wout, 0, D_MODEL), preferred_element_type=F32)
    return _rms(out, fnw_ref[...]) if final else out


def _causal_conv(ext, cw_ref, cb_ref, j):
    out = ext(0) * cw_ref[j, 0:1, :]
    for k in range(1, CONV_K):
        out = out + ext(k) * cw_ref[j, k:k + 1, :]
    return _silu(out + cb_ref[j:j + 1, :])


def _ssd_prompt_body(x_ref, params, masks, outs, scratch, *, tl, nt, i, j, final):
    nw_ref, wz, wxbc, wdt_ref, cw_ref, cb_ref, dtb_ref, alog_ref, dx_ref, normw_ref, wout, fnw_ref, r3_ref = params
    tril_ref, triu_ref, ones_ref, eye_ref = masks
    xo_ref, cst_ref, ssm_ref = outs
    cext_ref, act_ref, ybuf_ref, ht_ref, gate_ref = scratch
    t = pl.program_id(1)

    @pl.when(t == 0)
    def _():
        cext_ref[0:SUBLANES, :] = jnp.zeros((SUBLANES, CONV_DIM), F32)
        ht_ref[...] = jnp.zeros_like(ht_ref)

    h = _rms(x_ref[0], nw_ref[i:i + 1, :]).astype(BF16)
    cext_ref[SUBLANES:SUBLANES + tl, :] = jnp.dot(h, _wcols(wxbc, 0, CONV_DIM), preferred_element_type=F32)
    dt, a = _dt_and_decay(jnp.dot(h, wdt_ref[:, 0:N_HEADS], preferred_element_type=F32), dtb_ref, alog_ref, j)
    lo = SUBLANES - (CONV_K - 1)
    act_ref[...] = _causal_conv(lambda k: cext_ref[lo + k:lo + k + tl, :], cw_ref, cb_ref, j)

    tril, triu, ones, eye = tril_ref[...], triu_ref[...], ones_ref[...], eye_ref[...]
    n_chunks = tl // CHUNK
    zw = D_INNER // (n_chunks * N_GROUPS)
    for c in range(n_chunks):
        r0 = c * CHUNK
        rows = slice(r0, r0 + CHUNK)
        a_c, dt_c = a[rows], dt[rows]

        def gate_piece(g, c=c):
            z0 = (c * N_GROUPS + g) * zw
            gate_ref[:, z0:z0 + zw] = _silu(jnp.dot(h, _wcols(wz, z0, zw), preferred_element_type=F32))

        acum, alast = _ssd_block_diag(r0, a_c, dt_c, act_ref, ybuf_ref, tril, triu, ones, eye, CHUNK, gate_piece)
        scales = jnp.concatenate(
            [jnp.exp(acum), jnp.exp(alast - acum) * dt_c, jnp.exp(alast[0:SUBLANES])], axis=0)
        scales_x = _expand_heads(scales, r3_ref[...])
        ea_x = scales_x[0:CHUNK]
        dte_x = scales_x[CHUNK:2 * CHUNK]
        cd_x = scales_x[2 * CHUNK:2 * CHUNK + 1]
        for g in range(N_GROUPS):
            gcols = slice(g * GROUP_W, (g + 1) * GROUP_W)
            b0 = D_INNER + g * D_STATE
            c0 = D_INNER + N_GROUPS * D_STATE + g * D_STATE
            bg = act_ref[rows, b0:b0 + D_STATE].astype(BF16)
            cg = act_ref[rows, c0:c0 + D_STATE].astype(BF16)
            hprev = ht_ref[:, gcols]
            y_off = jnp.dot(cg, hprev.astype(BF16), preferred_element_type=F32) * ea_x[:, gcols]
            ybuf_ref[rows, gcols] = ybuf_ref[rows, gcols] + y_off
            xs = (act_ref[rows, gcols] * dte_x[:, gcols]).astype(BF16)
            st = lax.dot_general(bg, xs, (((0,), (0,)), ((), ())), preferred_element_type=F32)
            ht_ref[:, gcols] = hprev * cd_x[:, gcols] + st

    xo_ref[0] = _ssd_epilogue(x_ref[0], gate_ref[...], act_ref, ybuf_ref, dx_ref[j:j + 1, :], normw_ref[j:j + 1, :],
                              wout, fnw_ref, final)

    @pl.when(t == nt - 1)
    def _():
        cst_ref[0] = cext_ref[SUBLANES + tl - (CONV_K - 1):SUBLANES + tl, :]
        ssm_ref[0] = ht_ref[...].T

    cext_ref[0:SUBLANES, :] = cext_ref[tl:tl + SUBLANES, :]


def _ssd_sample_body(x_ref, cin_ref, hin_ref, params, masks, outs, scratch, *, bt, sb, seq, i, j, final):
    nw_ref, wz, wxbc, wdt_ref, cw_ref, cb_ref, dtb_ref, alog_ref, dx_ref, normw_ref, wout, fnw_ref, r3_ref = params
    tril_ref, triu_ref, ones_ref, eye_ref = masks
    xo_ref, cst_ref, hout_ref = outs
    cext_ref, act_ref, ybuf_ref, eax_ref, xsc_ref, cdt_ref, h_ref = scratch
    s_idx = pl.program_id(1)
    n_inner = bt // sb
    rows_total = bt * seq

    @pl.when(s_idx == 0)
    def _():
        h = _rms(x_ref[...], nw_ref[i:i + 1, :]).astype(BF16)
        h_ref[...] = h
        xbc = jnp.dot(h, _wcols(wxbc, 0, CONV_DIM), preferred_element_type=F32)
        dt, a = _dt_and_decay(jnp.dot(h, wdt_ref[:, 0:N_HEADS], preferred_element_type=F32), dtb_ref, alog_ref, j)
        lo = SUBLANES - (CONV_K - 1)
        cext_ref[:, 0:SUBLANES, :] = jnp.zeros((bt, SUBLANES, CONV_DIM), F32)
        cext_ref[:, lo:SUBLANES, :] = cin_ref[0]
        cext_ref[:, SUBLANES:SUBLANES + seq, :] = xbc.reshape(bt, seq, CONV_DIM)
        conv = _causal_conv(lambda k: cext_ref[:, lo + k:lo + k + seq, :], cw_ref, cb_ref, j)
        act_ref[...] = conv.reshape(rows_total, CONV_DIM)
        cst_ref[0] = cext_ref[:, SUBLANES + seq - (CONV_K - 1):SUBLANES + seq, :]
        tril, triu, ones, eye = tril_ref[...], triu_ref[...], ones_ref[...], eye_ref[...]
        for c in range(rows_total // CHUNK):
            r0 = c * CHUNK
            rows = slice(r0, r0 + CHUNK)
            a_c, dt_c = a[rows], dt[rows]
            acum, alast = _ssd_block_diag(r0, a_c, dt_c, act_ref, ybuf_ref, tril, triu, ones, eye, seq)
            scales = jnp.concatenate([jnp.exp(acum), jnp.exp(alast - acum) * dt_c], axis=0)
            scales_x = _expand_heads(scales, r3_ref[...])
            eax_ref[rows, :] = scales_x[0:CHUNK]
            xsc_ref[rows, :] = act_ref[rows, 0:D_INNER] * scales_x[CHUNK:2 * CHUNK]
            cdt_ref[:, rows] = jnp.exp(_dot01_tn(a_c, ones))

    for si in range(sb):
        r0 = pl.multiple_of((s_idx * sb + si) * seq, seq)
        rows = pl.ds(r0, seq)
        pick = (lax.broadcasted_iota(jnp.int32, (rows_total, D_STATE), 0) == r0).astype(BF16)
        cd = _dot01_r(cdt_ref[...], pick)
        for g in range(N_GROUPS):
            gcols = slice(g * GROUP_W, (g + 1) * GROUP_W)
            b0 = D_INNER + g * D_STATE
            c0 = D_INNER + N_GROUPS * D_STATE + g * D_STATE
            bg = act_ref[rows, b0:b0 + D_STATE].astype(BF16)
            cg = act_ref[rows, c0:c0 + D_STATE].astype(BF16)
            hprev = hin_ref[0, si, g * GROUP_W:(g + 1) * GROUP_W, :]
            y_off = lax.dot_general(cg, hprev.astype(BF16), (((1,), (1,)), ((), ())), preferred_element_type=F32)
            ybuf_ref[rows, gcols] = ybuf_ref[rows, gcols] + y_off * eax_ref[rows, gcols]
            st = lax.dot_general(xsc_ref[rows, gcols].astype(BF16), bg, (((0,), (0,)), ((), ())),
                                 preferred_element_type=F32)
            for e in range(HEADS_PER_GROUP):
                hh = g * HEADS_PER_GROUP + e
                hrows = slice(hh * HEAD_DIM, (hh + 1) * HEAD_DIM)
                hout_ref[0, si, hrows, :] = (hin_ref[0, si, hrows, :] * cd[hh:hh + 1, :]
                                             + st[e * HEAD_DIM:(e + 1) * HEAD_DIM, :])

    @pl.when(s_idx == n_inner - 1)
    def _():
        gate = _silu(jnp.dot(h_ref[...], _wcols(wz, 0, D_INNER), preferred_element_type=F32))
        xo_ref[...] = _ssd_epilogue(x_ref[...], gate, act_ref, ybuf_ref, dx_ref[j:j + 1, :], normw_ref[j:j + 1, :],
                                    wout, fnw_ref, final)


def _ssd_kernel(*refs, tl, nt, bt, sb, seq, i, j, final, n_prev):
    (xp_ref, xs_ref, cin_ref, hin_ref, *params, masks_p, masks_s, _, outs_p, outs_s, scratch_p, scratch_s) = _split(
        refs, None, None, None, None, *_SSD_PARAM_COUNTS, N_MASKS, N_MASKS, n_prev, 3, 3, 5, 7)
    _ssd_prompt_body(xp_ref, params, masks_p, outs_p, scratch_p, tl=tl, nt=nt, i=i, j=j, final=final)
    _ssd_sample_body(xs_ref, cin_ref, hin_ref, params, masks_s, outs_s, scratch_s, bt=bt, sb=sb, seq=seq, i=i, j=j,
                     final=final)


def _ssd_layer(xp, xs, state_conv, state_ssm, params, i, j, prev, *, seq, final):
    b, l, _ = xp.shape
    tl = SSD_TL
    nt = l // tl
    rows = xs.shape[0]
    nb = rows // seq
    bt = nb // b
    sb = bt // nt
    assert bt * b == nb and sb * nt == bt and (bt * seq) % CHUNK == 0
    hp = N_HEADS * HEAD_DIM
    p_operands, p_specs = _ssd_param_operands(params, j)
    masks = _ssd_block_masks(CHUNK) + _ssd_block_masks(seq)
    operands = [xp, xs, state_conv, state_ssm] + p_operands + masks
    prev = [] if prev is None else list(prev)
    aliases = {len(operands) + k: 4 + k for k in range(len(prev))}
    n_layers = state_ssm.shape[0]
    conv_spec = pl.BlockSpec((1, bt, CONV_K - 1, CONV_DIM), lambda s, t: (j, s, 0, 0))
    ssm_spec = pl.BlockSpec((1, sb, hp, D_STATE), lambda s, t: (j, s * nt + t, 0, 0))
    return pl.pallas_call(
        functools.partial(_ssd_kernel, tl=tl, nt=nt, bt=bt, sb=sb, seq=seq, i=i, j=j, final=final, n_prev=len(prev)),
        grid=(b, nt),
        in_specs=[
            pl.BlockSpec((1, tl, D_MODEL), lambda s, t: (s, t, 0)),
            pl.BlockSpec((bt * seq, D_MODEL), lambda s, t: (s, 0)),
            conv_spec,
            ssm_spec,
        ] + p_specs + [_const_spec(m.shape) for m in masks] + [pl.BlockSpec(memory_space=pl.ANY)] * len(prev),
        out_specs=[
            pl.BlockSpec((1, tl, D_MODEL), lambda s, t: (s, t, 0)),
            pl.BlockSpec((1, CONV_K - 1, CONV_DIM), lambda s, t: (s, 0, 0)),
            pl.BlockSpec((1, hp, D_STATE), lambda s, t: (s, 0, 0)),
            pl.BlockSpec((bt * seq, D_MODEL), lambda s, t: (s, 0)),
            conv_spec,
            ssm_spec,
        ],
        out_shape=[
            jax.ShapeDtypeStruct((b, l, D_MODEL), F32),
            jax.ShapeDtypeStruct((b, CONV_K - 1, CONV_DIM), F32),
            jax.ShapeDtypeStruct((b, hp, D_STATE), F32),
            jax.ShapeDtypeStruct((rows, D_MODEL), F32),
            jax.ShapeDtypeStruct((n_layers, nb, CONV_K - 1, CONV_DIM), F32),
            jax.ShapeDtypeStruct((n_layers, nb, hp, D_STATE), F32),
        ],
        input_output_aliases=aliases,
        scratch_shapes=[
            pltpu.VMEM((tl + SUBLANES, CONV_DIM), F32),
            pltpu.VMEM((tl, CONV_DIM), F32),
            pltpu.VMEM((tl, D_INNER), F32),
            pltpu.VMEM((D_STATE, hp), F32),
            pltpu.VMEM((tl, D_INNER), F32),
            pltpu.VMEM((bt, SUBLANES + seq, CONV_DIM), F32),
            pltpu.VMEM((bt * seq, CONV_DIM), F32),
            pltpu.VMEM((bt * seq, D_INNER), F32),
            pltpu.VMEM((bt * seq, D_INNER), F32),
            pltpu.VMEM((bt * seq, D_INNER), F32),
            pltpu.VMEM((N_HEADS, bt * seq), F32),
            pltpu.VMEM((bt * seq, D_MODEL), BF16),
        ],
        compiler_params=pltpu.CompilerParams(
            dimension_semantics=("arbitrary", "arbitrary"), vmem_limit_bytes=VMEM_LIMIT_BYTES),
        name="ssd_layer",
    )(*operands, *prev)


def kernel(x_prompt, x_sample, state_pool, state_conv, state_ssm, norm_w, pool_in_w, pool_mix_w, pool_scale,
           pool_out_w, ssd_in_w, ssd_conv_w, ssd_conv_b, ssd_dt_bias, ssd_A_log, ssd_D, ssd_norm_w, ssd_out_w,
           final_norm_w):
    nb, seq, _ = x_sample.shape
    hp = N_HEADS * HEAD_DIM
    xp = x_prompt
    xs = x_sample.reshape(nb * seq, D_MODEL)
    ssm_in = state_ssm.reshape(state_ssm.shape[0], nb, hp, D_STATE)
    pool_in = state_pool.transpose(0, 2, 1, 3)
    pool_params = (norm_w, pool_in_w.astype(BF16), pool_mix_w.astype(BF16), pool_scale, pool_out_w.astype(BF16))
    ssd_params = (norm_w, ssd_in_w.astype(BF16), ssd_conv_w, ssd_conv_b, ssd_dt_bias, ssd_A_log,
                  jnp.repeat(ssd_D, HEAD_DIM, axis=1), ssd_norm_w, ssd_out_w.astype(BF16),
                  final_norm_w.reshape(1, D_MODEL))
    pool_p, conv_p, ssm_p = [], [], []
    pool_s = ssd_s = None
    for i in range(DEPTH):
        j = i // 2
        if i % 2 == 0:
            xp, st = _pool_prompt(xp, pool_params, i, j)
            pool_p.append(st)
            xs, pool_s = _pool_sample(xs, pool_in, pool_params, i, j, pool_s, seq=seq)
        else:
            final = i == DEPTH - 1
            xp, cst, sst, xs, *ssd_s = _ssd_layer(xp, xs, state_conv, ssm_in, ssd_params, i, j, ssd_s, seq=seq,
                                                  final=final)
            conv_p.append(cst)
            ssm_p.append(sst.reshape(-1, N_HEADS, HEAD_DIM, D_STATE))
    conv_s, ssm_s = ssd_s
    return (xp, xs.reshape(nb, seq, D_MODEL), jnp.stack(pool_p), pool_s.transpose(0, 2, 1, 3), jnp.stack(conv_p), conv_s,
            jnp.stack(ssm_p), ssm_s.reshape(state_ssm.shape))
```

```python
import functools

import jax
import jax.numpy as jnp
from jax import lax
from jax.experimental import pallas as pl
from jax.experimental.pallas import tpu as pltpu

D_MODEL = 1024
DEPTH = 4
PAST_LEN = 16384
D_INNER = 2 * D_MODEL
POOL_WINDOWS = (2, 4, 8, 16)
POOL_GROUP = D_INNER // len(POOL_WINDOWS)
POOL_BUF = max(POOL_WINDOWS) - 1
HEAD_DIM = 64
N_HEADS = D_INNER // HEAD_DIM
D_STATE = 128
N_GROUPS = 4
HEADS_PER_GROUP = N_HEADS // N_GROUPS
GROUP_W = HEADS_PER_GROUP * HEAD_DIM
CONV_K = 4
CONV_DIM = D_INNER + 2 * N_GROUPS * D_STATE
CHUNK = 128
EPS = 1e-6

F32 = jnp.float32
BF16 = jnp.bfloat16

SUBLANES = 8
LANES = 128
SLAB = 512
VMEM_LIMIT_BYTES = 60 * 1024 * 1024

N_WIN = 2 * D_INNER // SLAB
N_WOUT = D_MODEL // SLAB
N_WZ = D_INNER // SLAB
N_WXBC = CONV_DIM // SLAB
DT_LANE_BLOCK = (D_INNER + CONV_DIM) // LANES

POOL_TL = 512
SSD_TL = 256
POOL_BT = 32


def _const_spec(shape):
    zeros = (0,) * len(shape)
    return pl.BlockSpec(shape, lambda *_: zeros, pipeline_mode=pl.Buffered(1))


def _layer_spec(shape, layer):
    idx = (layer,) + (0,) * len(shape)
    return pl.BlockSpec((None,) + tuple(shape), lambda *_: idx, pipeline_mode=pl.Buffered(1))


def _slab_specs(rows, layer, first, count):
    def spec(c):
        return pl.BlockSpec((None, rows, SLAB), lambda *_: (layer, 0, c), pipeline_mode=pl.Buffered(1))
    return [spec(first + c) for c in range(count)]


def _split(refs, *counts):
    out, pos = [], 0
    for n in counts:
        if n is None:
            out.append(refs[pos])
            pos += 1
        else:
            out.append(refs[pos:pos + n])
            pos += n
    assert pos == len(refs)
    return out


def _wcols(slabs, c0, width, krows=slice(None)):
    s0, off = divmod(c0, SLAB)
    if width <= SLAB:
        return slabs[s0][krows, off:off + width]
    return jnp.concatenate([slabs[s0 + i][krows, :] for i in range(width // SLAB)], axis=1)


def _rms(x, w):
    return x * lax.rsqrt(jnp.mean(x * x, axis=-1, keepdims=True) + EPS) * w


def _silu(x):
    half = 0.5 * x
    return half + half * jnp.tanh(half)


def _softplus(x):
    return jnp.maximum(x, 0.0) + jnp.log1p(jnp.exp(-jnp.abs(x)))


def _split3(v):
    v1 = v.astype(BF16).astype(F32)
    r1 = v - v1
    v2 = r1.astype(BF16).astype(F32)
    v3 = (r1 - v2).astype(BF16).astype(F32)
    return v1, v2, v3


def _sum3(f, v):
    p1, p2, p3 = (f(p.astype(BF16)) for p in _split3(v))
    return p1 + p2 + p3


def _dot01(m01, v):
    return _sum3(lambda p: jnp.dot(m01, p, preferred_element_type=F32), v)


def _dot01_tn(v, m01):
    dn = (((0,), (0,)), ((), ()))
    return _sum3(lambda p: lax.dot_general(p, m01, dn, preferred_element_type=F32), v)


def _dot01_r(v, m01):
    return _sum3(lambda p: jnp.dot(p, m01, preferred_element_type=F32), v)


def _expand_heads(v, r3):
    parts = jnp.concatenate(_split3(v), axis=1).astype(BF16)
    return jnp.dot(parts, r3, preferred_element_type=F32)


_POOL_WEIGHT_COUNTS = (None, N_WIN, None, None, N_WOUT)


def _pool_weight_operands(params, i, j):
    norm_w, win, wmix, scale, wout = params
    operands = [norm_w] + [win] * N_WIN + [wmix, scale] + [wout] * N_WOUT
    specs = ([_const_spec(norm_w.shape)] + _slab_specs(D_MODEL, j, 0, N_WIN)
             + [_layer_spec(wmix.shape[1:], j), _const_spec(scale.shape)] + _slab_specs(D_INNER, j, 0, N_WOUT))
    return operands, specs


def _reorder_rows(perm, v):
    return jnp.dot(perm, v, preferred_element_type=F32).astype(BF16)


def _pool_tail(p_of_group, z_of_group, wmix_ref, scale, wout, unperm=None):
    acc = None
    for g in range(len(POOL_WINDOWS)):
        cols = slice(g * POOL_GROUP, (g + 1) * POOL_GROUP)
        mixed = jnp.dot(p_of_group(g).astype(BF16), wmix_ref[g], preferred_element_type=F32)
        y = (mixed * scale[:, cols] * _silu(z_of_group(g))).astype(BF16)
        if unperm is not None:
            y = _reorder_rows(unperm, y)
        part = jnp.dot(y, _wcols(wout, 0, D_MODEL, cols), preferred_element_type=F32)
        acc = part if acc is None else acc + part
    return acc


def _pool_prompt_kernel(*refs, tl, nt, i, j):
    x_ref, nw_ref, win, wmix_ref, scale_ref, wout, xo_ref, st_ref, ext_ref = _split(
        refs, None, *_POOL_WEIGHT_COUNTS, None, None, None)
    t = pl.program_id(1)
    hist = 2 * SUBLANES
    x = x_ref[0]
    h = _rms(x, nw_ref[i:i + 1, :]).astype(BF16)

    @pl.when(t == 0)
    def _():
        ext_ref[0:hist, :] = jnp.zeros((hist, D_INNER), F32)

    ext_ref[hist:hist + tl, :] = jnp.dot(h, _wcols(win, 0, D_INNER), preferred_element_type=F32)
    pos = t * tl + lax.broadcasted_iota(jnp.int32, (tl, 1), 0)

    def p_of_group(g):
        w = POOL_WINDOWS[g]
        cols = slice(g * POOL_GROUP, (g + 1) * POOL_GROUP)
        u = ext_ref[hist:hist + tl, cols]
        s = u
        for k in range(1, w):
            s = s + ext_ref[hist - k:hist - k + tl, cols]
        cnt = jnp.minimum(pos + 1, w).astype(F32)
        return s * (1.0 / cnt) - u

    def z_of_group(g):
        return jnp.dot(h, _wcols(win, D_INNER + g * POOL_GROUP, POOL_GROUP), preferred_element_type=F32)

    xo_ref[0] = x + _pool_tail(p_of_group, z_of_group, wmix_ref, scale_ref[j:j + 1, :], wout)

    @pl.when(t == nt - 1)
    def _():
        st_ref[0] = ext_ref[hist + tl - POOL_BUF:hist + tl, :]

    ext_ref[0:hist, :] = ext_ref[tl:tl + hist, :]


def _pool_prompt(x, params, i, j):
    b, l, _ = x.shape
    tl = POOL_TL
    nt = l // tl
    w_operands, w_specs = _pool_weight_operands(params, i, j)
    return pl.pallas_call(
        functools.partial(_pool_prompt_kernel, tl=tl, nt=nt, i=i, j=j),
        grid=(b, nt),
        in_specs=[pl.BlockSpec((1, tl, D_MODEL), lambda s, t: (s, t, 0))] + w_specs,
        out_specs=[
            pl.BlockSpec((1, tl, D_MODEL), lambda s, t: (s, t, 0)),
            pl.BlockSpec((1, POOL_BUF, D_INNER), lambda s, t: (s, 0, 0)),
        ],
        out_shape=[
            jax.ShapeDtypeStruct((b, l, D_MODEL), F32),
            jax.ShapeDtypeStruct((b, POOL_BUF, D_INNER), F32),
        ],
        scratch_shapes=[pltpu.VMEM((tl + 2 * SUBLANES, D_INNER), F32)],
        compiler_params=pltpu.CompilerParams(
            dimension_semantics=("arbitrary", "arbitrary"), vmem_limit_bytes=VMEM_LIMIT_BYTES),
        name="pool_prompt",
    )(x, *w_operands)


def _pool_sample_kernel(*refs, bt, seq, start, i, j, n_prev):
    x_ref, buf_ref, perm_ref, unperm_ref, nw_ref, win, wmix_ref, scale_ref, wout, _, xo_ref, st_ref, u_ref = _split(
        refs, None, None, None, None, *_POOL_WEIGHT_COUNTS, n_prev, None, None, None)
    x = x_ref[...]
    h = _reorder_rows(perm_ref[...], _rms(x, nw_ref[i:i + 1, :]).astype(BF16))
    u_ref[...] = jnp.dot(h, _wcols(win, 0, D_INNER), preferred_element_type=F32).reshape(seq, bt, D_INNER)

    def src(t, cols):
        return u_ref[t, :, cols] if t >= 0 else buf_ref[0, POOL_BUF + t, :, cols]

    def p_of_group(g):
        w = POOL_WINDOWS[g]
        cols = slice(g * POOL_GROUP, (g + 1) * POOL_GROUP)
        outs = []
        for t in range(seq):
            cur = src(t, cols)
            s = cur
            for k in range(1, w):
                s = s + src(t - k, cols)
            outs.append(s * (1.0 / min(start + t + 1, w)) - cur)
        return jnp.concatenate(outs, axis=0)

    def z_of_group(g):
        return jnp.dot(h, _wcols(win, D_INNER + g * POOL_GROUP, POOL_GROUP), preferred_element_type=F32)

    xo_ref[...] = x + _pool_tail(p_of_group, z_of_group, wmix_ref, scale_ref[j:j + 1, :], wout, unperm_ref[...])
    keep = POOL_BUF - seq
    st_ref[0, 0:keep] = buf_ref[0, seq:POOL_BUF]
    st_ref[0, keep:POOL_BUF] = u_ref[...]


def _token_major_perm(bt, seq):
    r = jnp.arange(bt * seq)
    perm = (r[None, :] == ((r % bt) * seq + r // bt)[:, None]).astype(BF16)
    return perm, perm.T


def _pool_sample(x, state, params, i, j, prev, *, seq):
    rows = x.shape[0]
    nb = rows // seq
    bt = POOL_BT
    w_operands, w_specs = _pool_weight_operands(params, i, j)
    perms = _token_major_perm(bt, seq)
    operands = [x, state, *perms] + w_operands
    prev = [] if prev is None else [prev]
    aliases = {len(operands) + k: 1 + k for k in range(len(prev))}
    state_spec = pl.BlockSpec((1, POOL_BUF, bt, D_INNER), lambda s: (j, 0, s, 0))
    return pl.pallas_call(
        functools.partial(_pool_sample_kernel, bt=bt, seq=seq, start=PAST_LEN, i=i, j=j, n_prev=len(prev)),
        grid=(nb // bt,),
        in_specs=[pl.BlockSpec((bt * seq, D_MODEL), lambda s: (s, 0)), state_spec]
        + [_const_spec(p.shape) for p in perms] + w_specs + [pl.BlockSpec(memory_space=pl.ANY)] * len(prev),
        out_specs=[pl.BlockSpec((bt * seq, D_MODEL), lambda s: (s, 0)), state_spec],
        out_shape=[jax.ShapeDtypeStruct((rows, D_MODEL), F32), jax.ShapeDtypeStruct(state.shape, F32)],
        input_output_aliases=aliases,
        scratch_shapes=[pltpu.VMEM((seq, bt, D_INNER), F32)],
        compiler_params=pltpu.CompilerParams(
            dimension_semantics=("arbitrary",), vmem_limit_bytes=VMEM_LIMIT_BYTES),
        name="pool_sample",
    )(*operands, *prev)


_SSD_PARAM_COUNTS = (None, N_WZ, N_WXBC) + (None,) * 7 + (N_WOUT,) + (None,) * 2
N_MASKS = 4


def _ssd_block_masks(q):
    i = jnp.arange(CHUNK)[:, None]
    j = jnp.arange(CHUNK)[None, :]
    same = (i // q) == (j // q)
    tril = (same & (j <= i)).astype(BF16)
    ones = same.astype(BF16)
    eye = (i == j).astype(BF16)
    return [tril, tril.T, ones, eye]


def _head_expander():
    h = jnp.arange(N_HEADS)[:, None]
    c = jnp.arange(D_INNER)[None, :] // HEAD_DIM
    r = (h == c).astype(BF16)
    return jnp.concatenate([r, r, r], axis=0)


def _ssd_param_operands(params, j):
    norm_w, w_in, conv_w, conv_b, dt_bias, a_log, d_cols, ssd_norm_w, w_out, final_norm_w = params
    small = [conv_w, conv_b, dt_bias, a_log, d_cols, ssd_norm_w]
    tail = [final_norm_w, _head_expander()]
    operands = [norm_w] + [w_in] * (N_WZ + N_WXBC + 1) + small + [w_out] * N_WOUT + tail
    dt_spec = pl.BlockSpec((None, D_MODEL, LANES), lambda *_: (j, 0, DT_LANE_BLOCK), pipeline_mode=pl.Buffered(1))
    specs = ([_const_spec(norm_w.shape)] + _slab_specs(D_MODEL, j, 0, N_WZ) + _slab_specs(D_MODEL, j, N_WZ, N_WXBC)
             + [dt_spec] + [_const_spec(a.shape) for a in small] + _slab_specs(D_INNER, j, 0, N_WOUT)
             + [_const_spec(a.shape) for a in tail])
    return operands, specs


def _dt_and_decay(dt_raw, dtb_ref, alog_ref, j):
    dt = _softplus(dt_raw + dtb_ref[j:j + 1, :])
    return dt, dt * (-jnp.exp(alog_ref[j:j + 1, :]))


def _ssd_block_diag(r0, a_c, dt_c, act_ref, ybuf_ref, tril, triu, ones, eye, q, side_work=None):
    acum = _dot01(tril, a_c)
    acum_t = _dot01_tn(a_c, triu)
    alast = _dot01(ones, a_c)
    dt_t = _dot01_tn(dt_c, eye)
    ii = lax.broadcasted_iota(jnp.int32, (CHUNK, CHUNK), 0)
    jj = lax.broadcasted_iota(jnp.int32, (CHUNK, CHUNK), 1)
    causal = jj <= ii
    if q != CHUNK:
        causal = causal & ((ii // q) == (jj // q))
    rows = slice(r0, r0 + CHUNK)
    first_of_pair = lax.broadcasted_iota(jnp.int32, (CHUNK, 2 * HEAD_DIM), 1) < HEAD_DIM
    for g in range(N_GROUPS):
        if side_work is not None:
            side_work(g)
        b0 = D_INNER + g * D_STATE
        c0 = D_INNER + N_GROUPS * D_STATE + g * D_STATE
        bg = act_ref[rows, b0:b0 + D_STATE].astype(BF16)
        cg = act_ref[rows, c0:c0 + D_STATE].astype(BF16)
        cb = lax.dot_general(cg, bg, (((1,), (1,)), ((), ())), preferred_element_type=F32)
        for pair in range(HEADS_PER_GROUP // 2):
            h0 = g * HEADS_PER_GROUP + 2 * pair
            ws = []
            for hh in (h0, h0 + 1):
                seg = acum[:, hh:hh + 1] - acum_t[hh:hh + 1, :]
                decay = jnp.exp(jnp.where(causal, seg, -jnp.inf))
                ws.append((cb * decay * dt_t[hh:hh + 1, :]).astype(BF16))
            pcols = slice(h0 * HEAD_DIM, (h0 + 2) * HEAD_DIM)
            xp = act_ref[rows, pcols]
            rhs = jnp.concatenate([jnp.where(first_of_pair, xp, 0.0), jnp.where(first_of_pair, 0.0, xp)], axis=0)
            ybuf_ref[rows, pcols] = jnp.dot(jnp.concatenate(ws, axis=1), rhs.astype(BF16),
                                            preferred_element_type=F32)
    return acum, alast


def _ssd_epilogue(x, gate, act_ref, ybuf_ref, d_cols, norm_w, wout, fnw_ref, final):
    y = ybuf_ref[...] + d_cols * act_ref[:, :D_INNER]
    y = _rms(y * gate, norm_w)
    out = x + jnp.dot(y.astype(BF16), _wcols(wout, 0, D_MODEL), preferred_element_type=F32)
    return _rms(out, fnw_ref[...]) if final else out


def _causal_conv(ext, cw_ref, cb_ref, j, cols=slice(None)):
    out = ext(0) * cw_ref[j, 0:1, cols]
    for k in range(1, CONV_K):
        out = out + ext(k) * cw_ref[j, k:k + 1, cols]
    return _silu(out + cb_ref[j:j + 1, cols])


def _ssd_prompt_body(x_ref, params, masks, outs, scratch, *, tl, nt, i, j, final):
    nw_ref, wz, wxbc, wdt_ref, cw_ref, cb_ref, dtb_ref, alog_ref, dx_ref, normw_ref, wout, fnw_ref, r3_ref = params
    tril_ref, triu_ref, ones_ref, eye_ref = masks
    xo_ref, cst_ref, ssm_ref = outs
    cext_ref, act_ref, ybuf_ref, ht_ref, gate_ref = scratch
    t = pl.program_id(1)

    @pl.when(t == 0)
    def _():
        cext_ref[0:SUBLANES, :] = jnp.zeros((SUBLANES, CONV_DIM), F32)
        ht_ref[...] = jnp.zeros_like(ht_ref)

    h = _rms(x_ref[0], nw_ref[i:i + 1, :]).astype(BF16)
    dt, a = _dt_and_decay(jnp.dot(h, wdt_ref[:, 0:N_HEADS], preferred_element_type=F32), dtb_ref, alog_ref, j)
    lo = SUBLANES - (CONV_K - 1)
    for sl in range(N_WXBC):
        cols = slice(sl * SLAB, (sl + 1) * SLAB)
        cext_ref[SUBLANES:SUBLANES + tl, cols] = jnp.dot(h, wxbc[sl][...], preferred_element_type=F32)
        act_ref[:, cols] = _causal_conv(lambda k: cext_ref[lo + k:lo + k + tl, cols], cw_ref, cb_ref, j, cols)

    tril, triu, ones, eye = tril_ref[...], triu_ref[...], ones_ref[...], eye_ref[...]
    n_chunks = tl // CHUNK
    zw = D_INNER // (n_chunks * N_GROUPS)
    for c in range(n_chunks):
        r0 = c * CHUNK
        rows = slice(r0, r0 + CHUNK)
        a_c, dt_c = a[rows], dt[rows]

        def gate_piece(g, c=c):
            z0 = (c * N_GROUPS + g) * zw
            gate_ref[:, z0:z0 + zw] = _silu(jnp.dot(h, _wcols(wz, z0, zw), preferred_element_type=F32))

        acum, alast = _ssd_block_diag(r0, a_c, dt_c, act_ref, ybuf_ref, tril, triu, ones, eye, CHUNK, gate_piece)
        scales = jnp.concatenate(
            [jnp.exp(acum), jnp.exp(alast - acum) * dt_c, jnp.exp(alast[0:SUBLANES])], axis=0)
        scales_x = _expand_heads(scales, r3_ref[...])
        ea_x = scales_x[0:CHUNK]
        dte_x = scales_x[CHUNK:2 * CHUNK]
        cd_x = scales_x[2 * CHUNK:2 * CHUNK + 1]
        for g in range(N_GROUPS):
            gcols = slice(g * GROUP_W, (g + 1) * GROUP_W)
            b0 = D_INNER + g * D_STATE
            c0 = D_INNER + N_GROUPS * D_STATE + g * D_STATE
            bg = act_ref[rows, b0:b0 + D_STATE].astype(BF16)
            cg = act_ref[rows, c0:c0 + D_STATE].astype(BF16)
            hprev = ht_ref[:, gcols]
            y_off = jnp.dot(cg, hprev.astype(BF16), preferred_element_type=F32) * ea_x[:, gcols]
            ybuf_ref[rows, gcols] = ybuf_ref[rows, gcols] + y_off
            xs = (act_ref[rows, gcols] * dte_x[:, gcols]).astype(BF16)
            st = lax.dot_general(bg, xs, (((0,), (0,)), ((), ())), preferred_element_type=F32)
            ht_ref[:, gcols] = hprev * cd_x[:, gcols] + st

    xo_ref[0] = _ssd_epilogue(x_ref[0], gate_ref[...], act_ref, ybuf_ref, dx_ref[j:j + 1, :], normw_ref[j:j + 1, :],
                              wout, fnw_ref, final)

    @pl.when(t == nt - 1)
    def _():
        cst_ref[0] = cext_ref[SUBLANES + tl - (CONV_K - 1):SUBLANES + tl, :]
        ssm_ref[0, 0] = ht_ref[...].T

    cext_ref[0:SUBLANES, :] = cext_ref[tl:tl + SUBLANES, :]


def _ssd_sample_body(x_ref, cin_ref, hin_ref, params, masks, outs, scratch, *, bt, sb, seq, i, j, final):
    nw_ref, wz, wxbc, wdt_ref, cw_ref, cb_ref, dtb_ref, alog_ref, dx_ref, normw_ref, wout, fnw_ref, r3_ref = params
    tril_ref, triu_ref, ones_ref, eye_ref = masks
    xo_ref, cst_ref, hout_ref = outs
    cext_ref, act_ref, ybuf_ref, eax_ref, xsc_ref, cdt_ref, h_ref = scratch
    s_idx = pl.program_id(1)
    n_inner = bt // sb
    rows_total = bt * seq

    @pl.when(s_idx == 0)
    def _():
        h = _rms(x_ref[...], nw_ref[i:i + 1, :]).astype(BF16)
        h_ref[...] = h
        xbc = jnp.dot(h, _wcols(wxbc, 0, CONV_DIM), preferred_element_type=F32)
        dt, a = _dt_and_decay(jnp.dot(h, wdt_ref[:, 0:N_HEADS], preferred_element_type=F32), dtb_ref, alog_ref, j)
        lo = SUBLANES - (CONV_K - 1)
        cext_ref[:, 0:SUBLANES, :] = jnp.zeros((bt, SUBLANES, CONV_DIM), F32)
        cext_ref[:, lo:SUBLANES, :] = cin_ref[0]
        cext_ref[:, SUBLANES:SUBLANES + seq, :] = xbc.reshape(bt, seq, CONV_DIM)
        conv = _causal_conv(lambda k: cext_ref[:, lo + k:lo + k + seq, :], cw_ref, cb_ref, j)
        act_ref[...] = conv.reshape(rows_total, CONV_DIM)
        cst_ref[0] = cext_ref[:, SUBLANES + seq - (CONV_K - 1):SUBLANES + seq, :]
        tril, triu, ones, eye = tril_ref[...], triu_ref[...], ones_ref[...], eye_ref[...]
        for c in range(rows_total // CHUNK):
            r0 = c * CHUNK
            rows = slice(r0, r0 + CHUNK)
            a_c, dt_c = a[rows], dt[rows]
            acum, alast = _ssd_block_diag(r0, a_c, dt_c, act_ref, ybuf_ref, tril, triu, ones, eye, seq)
            scales = jnp.concatenate([jnp.exp(acum), jnp.exp(alast - acum) * dt_c], axis=0)
            scales_x = _expand_heads(scales, r3_ref[...])
            eax_ref[rows, :] = scales_x[0:CHUNK]
            xsc_ref[rows, :] = act_ref[rows, 0:D_INNER] * scales_x[CHUNK:2 * CHUNK]
            cdt_ref[:, rows] = jnp.exp(_dot01_tn(a_c, ones))

    for si in range(sb):
        r0 = pl.multiple_of((s_idx * sb + si) * seq, seq)
        rows = pl.ds(r0, seq)
        pick = (lax.broadcasted_iota(jnp.int32, (rows_total, D_STATE), 0) == r0).astype(BF16)
        cd = _dot01_r(cdt_ref[...], pick)
        for g in range(N_GROUPS):
            gcols = slice(g * GROUP_W, (g + 1) * GROUP_W)
            b0 = D_INNER + g * D_STATE
            c0 = D_INNER + N_GROUPS * D_STATE + g * D_STATE
            bg = act_ref[rows, b0:b0 + D_STATE].astype(BF16)
            cg = act_ref[rows, c0:c0 + D_STATE].astype(BF16)
            hprev = hin_ref[0, si, g * GROUP_W:(g + 1) * GROUP_W, :]
            y_off = lax.dot_general(cg, hprev.astype(BF16), (((1,), (1,)), ((), ())), preferred_element_type=F32)
            ybuf_ref[rows, gcols] = ybuf_ref[rows, gcols] + y_off * eax_ref[rows, gcols]
            st = lax.dot_general(xsc_ref[rows, gcols].astype(BF16), bg, (((0,), (0,)), ((), ())),
                                 preferred_element_type=F32)
            for e in range(HEADS_PER_GROUP):
                hh = g * HEADS_PER_GROUP + e
                hrows = slice(hh * HEAD_DIM, (hh + 1) * HEAD_DIM)
                hout_ref[0, si, hrows, :] = (hin_ref[0, si, hrows, :] * cd[hh:hh + 1, :]
                                             + st[e * HEAD_DIM:(e + 1) * HEAD_DIM, :])

    @pl.when(s_idx == n_inner - 1)
    def _():
        gate = _silu(jnp.dot(h_ref[...], _wcols(wz, 0, D_INNER), preferred_element_type=F32))
        xo_ref[...] = _ssd_epilogue(x_ref[...], gate, act_ref, ybuf_ref, dx_ref[j:j + 1, :], normw_ref[j:j + 1, :],
                                    wout, fnw_ref, final)


def _ssd_kernel(*refs, tl, nt, bt, sb, seq, i, j, final, n_prev):
    (xp_ref, xs_ref, cin_ref, hin_ref, *params, masks_p, masks_s, _, outs_p, outs_s, scratch_p, scratch_s) = _split(
        refs, None, None, None, None, *_SSD_PARAM_COUNTS, N_MASKS, N_MASKS, n_prev, 3, 3, 5, 7)
    _ssd_prompt_body(xp_ref, params, masks_p, outs_p, scratch_p, tl=tl, nt=nt, i=i, j=j, final=final)
    _ssd_sample_body(xs_ref, cin_ref, hin_ref, params, masks_s, outs_s, scratch_s, bt=bt, sb=sb, seq=seq, i=i, j=j,
                     final=final)


def _ssd_layer(xp, xs, state_conv, state_ssm, params, i, j, prev, *, seq, final):
    b, l, _ = xp.shape
    tl = SSD_TL
    nt = l // tl
    rows = xs.shape[0]
    nb = rows // seq
    bt = nb // b
    sb = bt // nt
    assert bt * b == nb and sb * nt == bt and (bt * seq) % CHUNK == 0
    hp = N_HEADS * HEAD_DIM
    p_operands, p_specs = _ssd_param_operands(params, j)
    masks = _ssd_block_masks(CHUNK) + _ssd_block_masks(seq)
    operands = [xp, xs, state_conv, state_ssm] + p_operands + masks
    prev = [] if prev is None else list(prev)
    aliases = {len(operands) + k: out for k, out in zip(range(len(prev)), (2, 4, 5))}
    n_layers = state_ssm.shape[0]
    conv_spec = pl.BlockSpec((1, bt, CONV_K - 1, CONV_DIM), lambda s, t: (j, s, 0, 0))
    ssm_spec = pl.BlockSpec((1, sb, hp, D_STATE), lambda s, t: (j, s * nt + t, 0, 0))
    xo, conv_p, ssm_p, xso, conv_s, ssm_s = pl.pallas_call(
        functools.partial(_ssd_kernel, tl=tl, nt=nt, bt=bt, sb=sb, seq=seq, i=i, j=j, final=final, n_prev=len(prev)),
        grid=(b, nt),
        in_specs=[
            pl.BlockSpec((1, tl, D_MODEL), lambda s, t: (s, t, 0)),
            pl.BlockSpec((bt * seq, D_MODEL), lambda s, t: (s, 0)),
            conv_spec,
            ssm_spec,
        ] + p_specs + [_const_spec(m.shape) for m in masks] + [pl.BlockSpec(memory_space=pl.ANY)] * len(prev),
        out_specs=[
            pl.BlockSpec((1, tl, D_MODEL), lambda s, t: (s, t, 0)),
            pl.BlockSpec((1, CONV_K - 1, CONV_DIM), lambda s, t: (s, 0, 0)),
            pl.BlockSpec((1, 1, hp, D_STATE), lambda s, t: (j, s, 0, 0)),
            pl.BlockSpec((bt * seq, D_MODEL), lambda s, t: (s, 0)),
            conv_spec,
            ssm_spec,
        ],
        out_shape=[
            jax.ShapeDtypeStruct((b, l, D_MODEL), F32),
            jax.ShapeDtypeStruct((b, CONV_K - 1, CONV_DIM), F32),
            jax.ShapeDtypeStruct((n_layers, b, hp, D_STATE), F32),
            jax.ShapeDtypeStruct((rows, D_MODEL), F32),
            jax.ShapeDtypeStruct((n_layers, nb, CONV_K - 1, CONV_DIM), F32),
            jax.ShapeDtypeStruct((n_layers, nb, hp, D_STATE), F32),
        ],
        input_output_aliases=aliases,
        scratch_shapes=[
            pltpu.VMEM((tl + SUBLANES, CONV_DIM), F32),
            pltpu.VMEM((tl, CONV_DIM), F32),
            pltpu.VMEM((tl, D_INNER), F32),
            pltpu.VMEM((D_STATE, hp), F32),
            pltpu.VMEM((tl, D_INNER), F32),
            pltpu.VMEM((bt, SUBLANES + seq, CONV_DIM), F32),
            pltpu.VMEM((bt * seq, CONV_DIM), F32),
            pltpu.VMEM((bt * seq, D_INNER), F32),
            pltpu.VMEM((bt * seq, D_INNER), F32),
            pltpu.VMEM((bt * seq, D_INNER), F32),
            pltpu.VMEM((N_HEADS, bt * seq), F32),
            pltpu.VMEM((bt * seq, D_MODEL), BF16),
        ],
        compiler_params=pltpu.CompilerParams(
            dimension_semantics=("arbitrary", "arbitrary"), vmem_limit_bytes=VMEM_LIMIT_BYTES),
        name="ssd_layer",
    )(*operands, *prev)
    return xo, conv_p, xso, (ssm_p, conv_s, ssm_s)


def kernel(x_prompt, x_sample, state_pool, state_conv, state_ssm, norm_w, pool_in_w, pool_mix_w, pool_scale,
           pool_out_w, ssd_in_w, ssd_conv_w, ssd_conv_b, ssd_dt_bias, ssd_A_log, ssd_D, ssd_norm_w, ssd_out_w,
           final_norm_w):
    nb, seq, _ = x_sample.shape
    hp = N_HEADS * HEAD_DIM
    xp = x_prompt
    xs = x_sample.reshape(nb * seq, D_MODEL)
    ssm_in = state_ssm.reshape(state_ssm.shape[0], nb, hp, D_STATE)
    pool_in = state_pool.transpose(0, 2, 1, 3)
    pool_params = (norm_w, pool_in_w.astype(BF16), pool_mix_w.astype(BF16), pool_scale, pool_out_w.astype(BF16))
    ssd_params = (norm_w, ssd_in_w.astype(BF16), ssd_conv_w, ssd_conv_b, ssd_dt_bias, ssd_A_log,
                  jnp.repeat(ssd_D, HEAD_DIM, axis=1), ssd_norm_w, ssd_out_w.astype(BF16),
                  final_norm_w.reshape(1, D_MODEL))
    pool_p, conv_p = [], []
    pool_s = ssd_s = None
    for i in range(DEPTH):
        j = i // 2
        if i % 2 == 0:
            xp, st = _pool_prompt(xp, pool_params, i, j)
            pool_p.append(st)
            xs, pool_s = _pool_sample(xs, pool_in, pool_params, i, j, pool_s, seq=seq)
        else:
            final = i == DEPTH - 1
            xp, cst, xs, ssd_s = _ssd_layer(xp, xs, state_conv, ssm_in, ssd_params, i, j, ssd_s, seq=seq, final=final)
            conv_p.append(cst)
    ssm_p, conv_s, ssm_s = ssd_s
    return (xp, xs.reshape(nb, seq, D_MODEL), jnp.stack(pool_p), pool_s.transpose(0, 2, 1, 3), jnp.stack(conv_p), conv_s,
            ssm_p.reshape((-1,) + x_prompt.shape[:1] + state_ssm.shape[2:]), ssm_s.reshape(state_ssm.shape))
```

```python
import functools

import jax
import jax.numpy as jnp
from jax import lax
from jax.experimental import pallas as pl
from jax.experimental.pallas import tpu as pltpu

D_MODEL = 1024
DEPTH = 4
PAST_LEN = 16384
D_INNER = 2 * D_MODEL
POOL_WINDOWS = (2, 4, 8, 16)
POOL_GROUP = D_INNER // len(POOL_WINDOWS)
POOL_BUF = max(POOL_WINDOWS) - 1
HEAD_DIM = 64
N_HEADS = D_INNER // HEAD_DIM
D_STATE = 128
N_GROUPS = 4
HEADS_PER_GROUP = N_HEADS // N_GROUPS
GROUP_W = HEADS_PER_GROUP * HEAD_DIM
CONV_K = 4
CONV_DIM = D_INNER + 2 * N_GROUPS * D_STATE
CHUNK = 128
EPS = 1e-6

F32 = jnp.float32
BF16 = jnp.bfloat16

SUBLANES = 8
LANES = 128
SLAB = 512
VMEM_LIMIT_BYTES = 60 * 1024 * 1024

N_WIN = 2 * D_INNER // SLAB
N_WOUT = D_MODEL // SLAB
N_WZ = D_INNER // SLAB
N_WXBC = CONV_DIM // SLAB
DT_LANE_BLOCK = (D_INNER + CONV_DIM) // LANES

POOL_TL = 512
SSD_TL = 256
POOL_BT = 32


def _const_spec(shape):
    zeros = (0,) * len(shape)
    return pl.BlockSpec(shape, lambda *_: zeros, pipeline_mode=pl.Buffered(1))


def _layer_spec(shape, layer):
    idx = (layer,) + (0,) * len(shape)
    return pl.BlockSpec((None,) + tuple(shape), lambda *_: idx, pipeline_mode=pl.Buffered(1))


def _slab_specs(rows, layer, first, count):
    def spec(c):
        return pl.BlockSpec((None, rows, SLAB), lambda *_: (layer, 0, c), pipeline_mode=pl.Buffered(1))
    return [spec(first + c) for c in range(count)]


def _split(refs, *counts):
    out, pos = [], 0
    for n in counts:
        if n is None:
            out.append(refs[pos])
            pos += 1
        else:
            out.append(refs[pos:pos + n])
            pos += n
    assert pos == len(refs)
    return out


def _wcols(slabs, c0, width, krows=slice(None)):
    s0, off = divmod(c0, SLAB)
    if width <= SLAB:
        return slabs[s0][krows, off:off + width]
    return jnp.concatenate([slabs[s0 + i][krows, :] for i in range(width // SLAB)], axis=1)


def _rms(x, w):
    return x * lax.rsqrt(jnp.mean(x * x, axis=-1, keepdims=True) + EPS) * w


def _silu(x):
    half = 0.5 * x
    return half + half * jnp.tanh(half)


def _softplus(x):
    return jnp.maximum(x, 0.0) + jnp.log1p(jnp.exp(-jnp.abs(x)))


def _split3(v):
    v1 = v.astype(BF16).astype(F32)
    r1 = v - v1
    v2 = r1.astype(BF16).astype(F32)
    v3 = (r1 - v2).astype(BF16).astype(F32)
    return v1, v2, v3


def _sum3(f, v):
    p1, p2, p3 = (f(p.astype(BF16)) for p in _split3(v))
    return p1 + p2 + p3


def _dot01(m01, v):
    return _sum3(lambda p: jnp.dot(m01, p, preferred_element_type=F32), v)


def _dot01_tn(v, m01):
    dn = (((0,), (0,)), ((), ()))
    return _sum3(lambda p: lax.dot_general(p, m01, dn, preferred_element_type=F32), v)


def _dot01_r(v, m01):
    return _sum3(lambda p: jnp.dot(p, m01, preferred_element_type=F32), v)


def _expand_heads(v, r3):
    parts = jnp.concatenate(_split3(v), axis=1).astype(BF16)
    return jnp.dot(parts, r3, preferred_element_type=F32)


_POOL_WEIGHT_COUNTS = (None, N_WIN, None, None, N_WOUT)


def _pool_weight_operands(params, i, j):
    norm_w, win, wmix, scale, wout = params
    operands = [norm_w] + [win] * N_WIN + [wmix, scale] + [wout] * N_WOUT
    specs = ([_const_spec(norm_w.shape)] + _slab_specs(D_MODEL, j, 0, N_WIN)
             + [_layer_spec(wmix.shape[1:], j), _const_spec(scale.shape)] + _slab_specs(D_INNER, j, 0, N_WOUT))
    return operands, specs


def _reorder_rows(perm, v):
    return jnp.dot(perm, v, preferred_element_type=F32).astype(BF16)


def _pool_tail(p_of_group, z_of_group, wmix_ref, scale, wout, unperm=None):
    acc = None
    for g in range(len(POOL_WINDOWS)):
        cols = slice(g * POOL_GROUP, (g + 1) * POOL_GROUP)
        mixed = jnp.dot(p_of_group(g).astype(BF16), wmix_ref[g], preferred_element_type=F32)
        y = (mixed * scale[:, cols] * _silu(z_of_group(g))).astype(BF16)
        if unperm is not None:
            y = _reorder_rows(unperm, y)
        part = jnp.dot(y, _wcols(wout, 0, D_MODEL, cols), preferred_element_type=F32)
        acc = part if acc is None else acc + part
    return acc


def _pool_prompt_kernel(*refs, tl, nt, i, j):
    x_ref, nw_ref, win, wmix_ref, scale_ref, wout, xo_ref, st_ref, ext_ref = _split(
        refs, None, *_POOL_WEIGHT_COUNTS, None, None, None)
    t = pl.program_id(1)
    hist = 2 * SUBLANES
    x = x_ref[0]

    @pl.when(t == 0)
    def _():
        ext_ref[0:hist, :] = jnp.zeros((hist, D_INNER), F32)

    half = tl // 2
    hs = []
    for r0 in (0, half):
        hs.append(_rms(x_ref[0, r0:r0 + half, :], nw_ref[i:i + 1, :]).astype(BF16))
        ext_ref[hist + r0:hist + r0 + half, :] = jnp.dot(hs[-1], _wcols(win, 0, D_INNER), preferred_element_type=F32)
    h = jnp.concatenate(hs, axis=0)
    pos = t * tl + lax.broadcasted_iota(jnp.int32, (tl, 1), 0)

    def p_of_group(g):
        w = POOL_WINDOWS[g]
        cols = slice(g * POOL_GROUP, (g + 1) * POOL_GROUP)
        u = ext_ref[hist:hist + tl, cols]
        s = u
        for k in range(1, w):
            s = s + ext_ref[hist - k:hist - k + tl, cols]
        cnt = jnp.minimum(pos + 1, w).astype(F32)
        return s * (1.0 / cnt) - u

    def z_of_group(g):
        return jnp.dot(h, _wcols(win, D_INNER + g * POOL_GROUP, POOL_GROUP), preferred_element_type=F32)

    xo_ref[0] = x + _pool_tail(p_of_group, z_of_group, wmix_ref, scale_ref[j:j + 1, :], wout)

    @pl.when(t == nt - 1)
    def _():
        st_ref[0] = ext_ref[hist + tl - POOL_BUF:hist + tl, :]

    ext_ref[0:hist, :] = ext_ref[tl:tl + hist, :]


def _pool_prompt(x, params, i, j):
    b, l, _ = x.shape
    tl = POOL_TL
    nt = l // tl
    w_operands, w_specs = _pool_weight_operands(params, i, j)
    return pl.pallas_call(
        functools.partial(_pool_prompt_kernel, tl=tl, nt=nt, i=i, j=j),
        grid=(b, nt),
        in_specs=[pl.BlockSpec((1, tl, D_MODEL), lambda s, t: (s, t, 0))] + w_specs,
        out_specs=[
            pl.BlockSpec((1, tl, D_MODEL), lambda s, t: (s, t, 0)),
            pl.BlockSpec((1, POOL_BUF, D_INNER), lambda s, t: (s, 0, 0)),
        ],
        out_shape=[
            jax.ShapeDtypeStruct((b, l, D_MODEL), F32),
            jax.ShapeDtypeStruct((b, POOL_BUF, D_INNER), F32),
        ],
        scratch_shapes=[pltpu.VMEM((tl + 2 * SUBLANES, D_INNER), F32)],
        compiler_params=pltpu.CompilerParams(
            dimension_semantics=("arbitrary", "arbitrary"), vmem_limit_bytes=VMEM_LIMIT_BYTES),
        name="pool_prompt",
    )(x, *w_operands)


def _pool_sample_kernel(*refs, bt, seq, start, i, j, n_prev):
    x_ref, buf_ref, perm_ref, unperm_ref, nw_ref, win, wmix_ref, scale_ref, wout, _, xo_ref, st_ref, u_ref = _split(
        refs, None, None, None, None, *_POOL_WEIGHT_COUNTS, n_prev, None, None, None)
    x = x_ref[...]
    h = _reorder_rows(perm_ref[...], _rms(x, nw_ref[i:i + 1, :]).astype(BF16))
    u_ref[...] = jnp.dot(h, _wcols(win, 0, D_INNER), preferred_element_type=F32).reshape(seq, bt, D_INNER)

    def src(t, cols):
        return u_ref[t, :, cols] if t >= 0 else buf_ref[0, POOL_BUF + t, :, cols]

    def p_of_group(g):
        w = POOL_WINDOWS[g]
        cols = slice(g * POOL_GROUP, (g + 1) * POOL_GROUP)
        outs = []
        for t in range(seq):
            cur = src(t, cols)
            s = cur
            for k in range(1, w):
                s = s + src(t - k, cols)
            outs.append(s * (1.0 / min(start + t + 1, w)) - cur)
        return jnp.concatenate(outs, axis=0)

    def z_of_group(g):
        return jnp.dot(h, _wcols(win, D_INNER + g * POOL_GROUP, POOL_GROUP), preferred_element_type=F32)

    xo_ref[...] = x + _pool_tail(p_of_group, z_of_group, wmix_ref, scale_ref[j:j + 1, :], wout, unperm_ref[...])
    keep = POOL_BUF - seq
    st_ref[0, 0:keep] = buf_ref[0, seq:POOL_BUF]
    st_ref[0, keep:POOL_BUF] = u_ref[...]


def _token_major_perm(bt, seq):
    r = jnp.arange(bt * seq)
    perm = (r[None, :] == ((r % bt) * seq + r // bt)[:, None]).astype(BF16)
    return perm, perm.T


def _pool_sample(x, state, params, i, j, prev, *, seq):
    rows = x.shape[0]
    nb = rows // seq
    bt = POOL_BT
    w_operands, w_specs = _pool_weight_operands(params, i, j)
    perms = _token_major_perm(bt, seq)
    operands = [x, state, *perms] + w_operands
    prev = [] if prev is None else [prev]
    aliases = {len(operands) + k: 1 + k for k in range(len(prev))}
    state_spec = pl.BlockSpec((1, POOL_BUF, bt, D_INNER), lambda s: (j, 0, s, 0))
    return pl.pallas_call(
        functools.partial(_pool_sample_kernel, bt=bt, seq=seq, start=PAST_LEN, i=i, j=j, n_prev=len(prev)),
        grid=(nb // bt,),
        in_specs=[pl.BlockSpec((bt * seq, D_MODEL), lambda s: (s, 0)), state_spec]
        + [_const_spec(p.shape) for p in perms] + w_specs + [pl.BlockSpec(memory_space=pl.ANY)] * len(prev),
        out_specs=[pl.BlockSpec((bt * seq, D_MODEL), lambda s: (s, 0)), state_spec],
        out_shape=[jax.ShapeDtypeStruct((rows, D_MODEL), F32), jax.ShapeDtypeStruct(state.shape, F32)],
        input_output_aliases=aliases,
        scratch_shapes=[pltpu.VMEM((seq, bt, D_INNER), F32)],
        compiler_params=pltpu.CompilerParams(
            dimension_semantics=("arbitrary",), vmem_limit_bytes=VMEM_LIMIT_BYTES),
        name="pool_sample",
    )(*operands, *prev)


_SSD_PARAM_COUNTS = (None, N_WZ, N_WXBC) + (None,) * 7 + (N_WOUT,) + (None,) * 2
N_MASKS = 4


def _ssd_block_masks(q):
    i = jnp.arange(CHUNK)[:, None]
    j = jnp.arange(CHUNK)[None, :]
    same = (i // q) == (j // q)
    tril = (same & (j <= i)).astype(BF16)
    ones = same.astype(BF16)
    eye = (i == j).astype(BF16)
    return [tril, tril.T, ones, eye]


def _head_expander():
    h = jnp.arange(N_HEADS)[:, None]
    c = jnp.arange(D_INNER)[None, :] // HEAD_DIM
    r = (h == c).astype(BF16)
    return jnp.concatenate([r, r, r], axis=0)


def _ssd_param_operands(params, j):
    norm_w, w_in, conv_w, conv_b, dt_bias, a_log, d_cols, ssd_norm_w, w_out, final_norm_w = params
    small = [conv_w, conv_b, dt_bias, a_log, d_cols, ssd_norm_w]
    tail = [final_norm_w, _head_expander()]
    operands = [norm_w] + [w_in] * (N_WZ + N_WXBC + 1) + small + [w_out] * N_WOUT + tail
    dt_spec = pl.BlockSpec((None, D_MODEL, LANES), lambda *_: (j, 0, DT_LANE_BLOCK), pipeline_mode=pl.Buffered(1))
    specs = ([_const_spec(norm_w.shape)] + _slab_specs(D_MODEL, j, 0, N_WZ) + _slab_specs(D_MODEL, j, N_WZ, N_WXBC)
             + [dt_spec] + [_const_spec(a.shape) for a in small] + _slab_specs(D_INNER, j, 0, N_WOUT)
             + [_const_spec(a.shape) for a in tail])
    return operands, specs


def _dt_and_decay(dt_raw, dtb_ref, alog_ref, j):
    dt = _softplus(dt_raw + dtb_ref[j:j + 1, :])
    return dt, dt * (-jnp.exp(alog_ref[j:j + 1, :]))


def _ssd_block_sums(a_c, dt_c, tril, triu, ones, eye):
    acum = _dot01(tril, a_c)
    acum_t = _dot01_tn(a_c, triu)
    alast = _dot01(ones, a_c)
    dt_t = _dot01_tn(dt_c, eye)
    return acum, acum_t, alast, dt_t


def _ssd_block_diag(r0, sums, act_ref, ybuf_ref, q, side_work=None, carried=None):
    acum, acum_t, _, dt_t = sums
    ii = lax.broadcasted_iota(jnp.int32, (CHUNK, CHUNK), 0)
    jj = lax.broadcasted_iota(jnp.int32, (CHUNK, CHUNK), 1)
    causal = jj <= ii
    if q != CHUNK:
        causal = causal & ((ii // q) == (jj // q))
    rows = slice(r0, r0 + CHUNK)
    first_of_pair = lax.broadcasted_iota(jnp.int32, (CHUNK, 2 * HEAD_DIM), 1) < HEAD_DIM
    for g in range(N_GROUPS):
        if side_work is not None:
            side_work(g)
        b0 = D_INNER + g * D_STATE
        c0 = D_INNER + N_GROUPS * D_STATE + g * D_STATE
        bg = act_ref[rows, b0:b0 + D_STATE].astype(BF16)
        cg = act_ref[rows, c0:c0 + D_STATE].astype(BF16)
        cb = lax.dot_general(cg, bg, (((1,), (1,)), ((), ())), preferred_element_type=F32)
        y_carried = None if carried is None else carried(g, bg, cg)
        for pair in range(HEADS_PER_GROUP // 2):
            h0 = g * HEADS_PER_GROUP + 2 * pair
            ws = []
            for hh in (h0, h0 + 1):
                seg = acum[:, hh:hh + 1] - acum_t[hh:hh + 1, :]
                decay = jnp.exp(jnp.where(causal, seg, -jnp.inf))
                ws.append((cb * decay * dt_t[hh:hh + 1, :]).astype(BF16))
            pcols = slice(h0 * HEAD_DIM, (h0 + 2) * HEAD_DIM)
            xp = act_ref[rows, pcols]
            rhs = jnp.concatenate([jnp.where(first_of_pair, xp, 0.0), jnp.where(first_of_pair, 0.0, xp)], axis=0)
            yd = jnp.dot(jnp.concatenate(ws, axis=1), rhs.astype(BF16), preferred_element_type=F32)
            if y_carried is not None:
                yd = yd + y_carried[:, 2 * pair * HEAD_DIM:2 * (pair + 1) * HEAD_DIM]
            ybuf_ref[rows, pcols] = yd


def _ssd_epilogue(x, gate, act_ref, ybuf_ref, d_cols, norm_w, wout, fnw_ref, final):
    y = ybuf_ref[...] + d_cols * act_ref[:, :D_INNER]
    y = _rms(y * gate, norm_w)
    out = x + jnp.dot(y.astype(BF16), _wcols(wout, 0, D_MODEL), preferred_element_type=F32)
    return _rms(out, fnw_ref[...]) if final else out


def _causal_conv(ext, cw_ref, cb_ref, j, cols=slice(None)):
    out = ext(0) * cw_ref[j, 0:1, cols]
    for k in range(1, CONV_K):
        out = out + ext(k) * cw_ref[j, k:k + 1, cols]
    return _silu(out + cb_ref[j:j + 1, cols])


def _ssd_prompt_phases(x_ref, params, masks, outs, scratch, *, tl, i, j, final):
    nw_ref, wz, wxbc, wdt_ref, cw_ref, cb_ref, dtb_ref, alog_ref, dx_ref, normw_ref, wout, fnw_ref, r3_ref = params
    tril_ref, triu_ref, ones_ref, eye_ref = masks
    xo_ref, cst_ref, ssm_ref = outs
    cext_ref, act_ref, ybuf_ref, ht_ref, gate_ref = scratch

    def first():
        cext_ref[0:SUBLANES, :] = jnp.zeros((SUBLANES, CONV_DIM), F32)
        ht_ref[...] = jnp.zeros_like(ht_ref)

    def last():
        cst_ref[0] = cext_ref[SUBLANES - (CONV_K - 1):SUBLANES, :]
        ssm_ref[0, 0] = ht_ref[...].T

    def main():
        _ssd_prompt_main(x_ref, nw_ref, wz, wxbc, wdt_ref, cw_ref, cb_ref, dtb_ref, alog_ref, dx_ref, normw_ref, wout,
                         fnw_ref, r3_ref, tril_ref, triu_ref, ones_ref, eye_ref, xo_ref, cext_ref, act_ref, ybuf_ref,
                         ht_ref, gate_ref, tl=tl, i=i, j=j, final=final)

    return first, main, last


def _ssd_prompt_main(x_ref, nw_ref, wz, wxbc, wdt_ref, cw_ref, cb_ref, dtb_ref, alog_ref, dx_ref, normw_ref, wout,
                     fnw_ref, r3_ref, tril_ref, triu_ref, ones_ref, eye_ref, xo_ref, cext_ref, act_ref, ybuf_ref,
                     ht_ref, gate_ref, *, tl, i, j, final):
    h = _rms(x_ref[0], nw_ref[i:i + 1, :]).astype(BF16)
    dt, a = _dt_and_decay(jnp.dot(h, wdt_ref[:, 0:N_HEADS], preferred_element_type=F32), dtb_ref, alog_ref, j)
    lo = SUBLANES - (CONV_K - 1)
    for sl in range(N_WXBC):
        cols = slice(sl * SLAB, (sl + 1) * SLAB)
        cext_ref[SUBLANES:SUBLANES + tl, cols] = jnp.dot(h, wxbc[sl][...], preferred_element_type=F32)
        act_ref[:, cols] = _causal_conv(lambda k: cext_ref[lo + k:lo + k + tl, cols], cw_ref, cb_ref, j, cols)

    tril, triu, ones, eye = tril_ref[...], triu_ref[...], ones_ref[...], eye_ref[...]
    n_chunks = tl // CHUNK
    zw = D_INNER // (n_chunks * N_GROUPS)
    for c in range(n_chunks):
        r0 = c * CHUNK
        rows = slice(r0, r0 + CHUNK)
        a_c, dt_c = a[rows], dt[rows]

        def gate_piece(g, c=c):
            z0 = (c * N_GROUPS + g) * zw
            gate_ref[:, z0:z0 + zw] = _silu(jnp.dot(h, _wcols(wz, z0, zw), preferred_element_type=F32))

        sums = _ssd_block_sums(a_c, dt_c, tril, triu, ones, eye)
        acum, _, alast, _ = sums
        scales = jnp.concatenate(
            [jnp.exp(acum), jnp.exp(alast - acum) * dt_c, jnp.exp(alast[0:SUBLANES])], axis=0)
        scales_x = _expand_heads(scales, r3_ref[...])

        def carried(g, bg, cg, rows=rows, scales_x=scales_x):
            gcols = slice(g * GROUP_W, (g + 1) * GROUP_W)
            hprev = ht_ref[:, gcols]
            y_off = jnp.dot(cg, hprev.astype(BF16), preferred_element_type=F32) * scales_x[0:CHUNK, gcols]
            xs = (act_ref[rows, gcols] * scales_x[CHUNK:2 * CHUNK, gcols]).astype(BF16)
            st = lax.dot_general(bg, xs, (((0,), (0,)), ((), ())), preferred_element_type=F32)
            ht_ref[:, gcols] = hprev * scales_x[2 * CHUNK:2 * CHUNK + 1, gcols] + st
            return y_off

        _ssd_block_diag(r0, sums, act_ref, ybuf_ref, CHUNK, gate_piece, carried)

    xo_ref[0] = _ssd_epilogue(x_ref[0], gate_ref[...], act_ref, ybuf_ref, dx_ref[j:j + 1, :], normw_ref[j:j + 1, :],
                              wout, fnw_ref, final)
    cext_ref[0:SUBLANES, :] = cext_ref[tl:tl + SUBLANES, :]


def _ssd_sample_phases(x_ref, cin_ref, hin_ref, params, masks, outs, scratch, *, bt, sb, seq, i, j, final):
    nw_ref, wz, wxbc, wdt_ref, cw_ref, cb_ref, dtb_ref, alog_ref, dx_ref, normw_ref, wout, fnw_ref, r3_ref = params
    tril_ref, triu_ref, ones_ref, eye_ref = masks
    xo_ref, cst_ref, hout_ref = outs
    cext_ref, act_ref, ybuf_ref, eax_ref, xsc_ref, cdt_ref, h_ref = scratch
    s_idx = pl.program_id(1)
    rows_total = bt * seq

    def first():
        h = _rms(x_ref[...], nw_ref[i:i + 1, :]).astype(BF16)
        h_ref[...] = h
        xbc = jnp.dot(h, _wcols(wxbc, 0, CONV_DIM), preferred_element_type=F32)
        dt, a = _dt_and_decay(jnp.dot(h, wdt_ref[:, 0:N_HEADS], preferred_element_type=F32), dtb_ref, alog_ref, j)
        lo = SUBLANES - (CONV_K - 1)
        cext_ref[:, 0:SUBLANES, :] = jnp.zeros((bt, SUBLANES, CONV_DIM), F32)
        cext_ref[:, lo:SUBLANES, :] = cin_ref[0]
        cext_ref[:, SUBLANES:SUBLANES + seq, :] = xbc.reshape(bt, seq, CONV_DIM)
        conv = _causal_conv(lambda k: cext_ref[:, lo + k:lo + k + seq, :], cw_ref, cb_ref, j)
        act_ref[...] = conv.reshape(rows_total, CONV_DIM)
        cst_ref[0] = cext_ref[:, SUBLANES + seq - (CONV_K - 1):SUBLANES + seq, :]
        tril, triu, ones, eye = tril_ref[...], triu_ref[...], ones_ref[...], eye_ref[...]
        for c in range(rows_total // CHUNK):
            r0 = c * CHUNK
            rows = slice(r0, r0 + CHUNK)
            a_c, dt_c = a[rows], dt[rows]
            sums = _ssd_block_sums(a_c, dt_c, tril, triu, ones, eye)
            acum, _, alast, _ = sums
            _ssd_block_diag(r0, sums, act_ref, ybuf_ref, seq)
            scales = jnp.concatenate([jnp.exp(acum), jnp.exp(alast - acum) * dt_c], axis=0)
            scales_x = _expand_heads(scales, r3_ref[...])
            eax_ref[rows, :] = scales_x[0:CHUNK]
            xsc_ref[rows, :] = act_ref[rows, 0:D_INNER] * scales_x[CHUNK:2 * CHUNK]
            cdt_ref[:, rows] = jnp.exp(_dot01_tn(a_c, ones))

    def main():
        for si in range(sb):
            _ssd_sample_sequence(si, pl.multiple_of((s_idx * sb + si) * seq, seq))

    def _ssd_sample_sequence(si, r0):
        rows = pl.ds(r0, seq)
        pick = (lax.broadcasted_iota(jnp.int32, (rows_total, D_STATE), 0) == r0).astype(BF16)
        cd = _dot01_r(cdt_ref[...], pick)
        for g in range(N_GROUPS):
            gcols = slice(g * GROUP_W, (g + 1) * GROUP_W)
            b0 = D_INNER + g * D_STATE
            c0 = D_INNER + N_GROUPS * D_STATE + g * D_STATE
            bg = act_ref[rows, b0:b0 + D_STATE].astype(BF16)
            cg = act_ref[rows, c0:c0 + D_STATE].astype(BF16)
            hprev = hin_ref[0, si, g * GROUP_W:(g + 1) * GROUP_W, :]
            y_off = lax.dot_general(cg, hprev.astype(BF16), (((1,), (1,)), ((), ())), preferred_element_type=F32)
            ybuf_ref[rows, gcols] = ybuf_ref[rows, gcols] + y_off * eax_ref[rows, gcols]
            st = lax.dot_general(xsc_ref[rows, gcols].astype(BF16), bg, (((0,), (0,)), ((), ())),
                                 preferred_element_type=F32)
            for e in range(HEADS_PER_GROUP):
                hh = g * HEADS_PER_GROUP + e
                hrows = slice(hh * HEAD_DIM, (hh + 1) * HEAD_DIM)
                hout_ref[0, si, hrows, :] = (hin_ref[0, si, hrows, :] * cd[hh:hh + 1, :]
                                             + st[e * HEAD_DIM:(e + 1) * HEAD_DIM, :])

    def last():
        gate = _silu(jnp.dot(h_ref[...], _wcols(wz, 0, D_INNER), preferred_element_type=F32))
        xo_ref[...] = _ssd_epilogue(x_ref[...], gate, act_ref, ybuf_ref, dx_ref[j:j + 1, :], normw_ref[j:j + 1, :],
                                    wout, fnw_ref, final)

    return first, main, last


def _ssd_kernel(*refs, tl, nt, bt, sb, seq, i, j, final, n_prev):
    (xp_ref, xs_ref, cin_ref, hin_ref, *params, masks_p, masks_s, _, outs_p, outs_s, scratch_p, scratch_s) = _split(
        refs, None, None, None, None, *_SSD_PARAM_COUNTS, N_MASKS, N_MASKS, n_prev, 3, 3, 5, 7)
    prompt = _ssd_prompt_phases(xp_ref, params, masks_p, outs_p, scratch_p, tl=tl, i=i, j=j, final=final)
    sample = _ssd_sample_phases(xs_ref, cin_ref, hin_ref, params, masks_s, outs_s, scratch_s, bt=bt, sb=sb, seq=seq,
                                i=i, j=j, final=final)
    t = pl.program_id(1)
    for when, phase in ((t == 0, 0), (None, 1), (t == nt - 1, 2)):
        def run(phase=phase):
            prompt[phase]()
            sample[phase]()
        run() if when is None else pl.when(when)(run)


def _ssd_layer(xp, xs, state_conv, state_ssm, params, i, j, prev, *, seq, final):
    b, l, _ = xp.shape
    tl = SSD_TL
    nt = l // tl
    rows = xs.shape[0]
    nb = rows // seq
    bt = nb // b
    sb = bt // nt
    assert bt * b == nb and sb * nt == bt and (bt * seq) % CHUNK == 0
    hp = N_HEADS * HEAD_DIM
    p_operands, p_specs = _ssd_param_operands(params, j)
    masks = _ssd_block_masks(CHUNK) + _ssd_block_masks(seq)
    operands = [xp, xs, state_conv, state_ssm] + p_operands + masks
    prev = [] if prev is None else list(prev)
    aliases = {len(operands) + k: out for k, out in zip(range(len(prev)), (2, 4, 5))}
    n_layers = state_ssm.shape[0]
    conv_spec = pl.BlockSpec((1, bt, CONV_K - 1, CONV_DIM), lambda s, t: (j, s, 0, 0))
    ssm_spec = pl.BlockSpec((1, sb, hp, D_STATE), lambda s, t: (j, s * nt + t, 0, 0))
    xo, conv_p, ssm_p, xso, conv_s, ssm_s = pl.pallas_call(
        functools.partial(_ssd_kernel, tl=tl, nt=nt, bt=bt, sb=sb, seq=seq, i=i, j=j, final=final, n_prev=len(prev)),
        grid=(b, nt),
        in_specs=[
            pl.BlockSpec((1, tl, D_MODEL), lambda s, t: (s, t, 0)),
            pl.BlockSpec((bt * seq, D_MODEL), lambda s, t: (s, 0)),
            conv_spec,
            ssm_spec,
        ] + p_specs + [_const_spec(m.shape) for m in masks] + [pl.BlockSpec(memory_space=pl.ANY)] * len(prev),
        out_specs=[
            pl.BlockSpec((1, tl, D_MODEL), lambda s, t: (s, t, 0)),
            pl.BlockSpec((1, CONV_K - 1, CONV_DIM), lambda s, t: (s, 0, 0)),
            pl.BlockSpec((1, 1, hp, D_STATE), lambda s, t: (j, s, 0, 0)),
            pl.BlockSpec((bt * seq, D_MODEL), lambda s, t: (s, 0)),
            conv_spec,
            ssm_spec,
        ],
        out_shape=[
            jax.ShapeDtypeStruct((b, l, D_MODEL), F32),
            jax.ShapeDtypeStruct((b, CONV_K - 1, CONV_DIM), F32),
            jax.ShapeDtypeStruct((n_layers, b, hp, D_STATE), F32),
            jax.ShapeDtypeStruct((rows, D_MODEL), F32),
            jax.ShapeDtypeStruct((n_layers, nb, CONV_K - 1, CONV_DIM), F32),
            jax.ShapeDtypeStruct((n_layers, nb, hp, D_STATE), F32),
        ],
        input_output_aliases=aliases,
        scratch_shapes=[
            pltpu.VMEM((tl + SUBLANES, CONV_DIM), F32),
            pltpu.VMEM((tl, CONV_DIM), F32),
            pltpu.VMEM((tl, D_INNER), F32),
            pltpu.VMEM((D_STATE, hp), F32),
            pltpu.VMEM((tl, D_INNER), F32),
            pltpu.VMEM((bt, SUBLANES + seq, CONV_DIM), F32),
            pltpu.VMEM((bt * seq, CONV_DIM), F32),
            pltpu.VMEM((bt * seq, D_INNER), F32),
            pltpu.VMEM((bt * seq, D_INNER), F32),
            pltpu.VMEM((bt * seq, D_INNER), F32),
            pltpu.VMEM((N_HEADS, bt * seq), F32),
            pltpu.VMEM((bt * seq, D_MODEL), BF16),
        ],
        compiler_params=pltpu.CompilerParams(
            dimension_semantics=("arbitrary", "arbitrary"), vmem_limit_bytes=VMEM_LIMIT_BYTES),
        name="ssd_layer",
    )(*operands, *prev)
    return xo, conv_p, xso, (ssm_p, conv_s, ssm_s)


def kernel(x_prompt, x_sample, state_pool, state_conv, state_ssm, norm_w, pool_in_w, pool_mix_w, pool_scale,
           pool_out_w, ssd_in_w, ssd_conv_w, ssd_conv_b, ssd_dt_bias, ssd_A_log, ssd_D, ssd_norm_w, ssd_out_w,
           final_norm_w):
    nb, seq, _ = x_sample.shape
    hp = N_HEADS * HEAD_DIM
    xp = x_prompt
    xs = x_sample.reshape(nb * seq, D_MODEL)
    ssm_in = state_ssm.reshape(state_ssm.shape[0], nb, hp, D_STATE)
    pool_in = state_pool.transpose(0, 2, 1, 3)
    pool_params = (norm_w, pool_in_w.astype(BF16), pool_mix_w.astype(BF16), pool_scale, pool_out_w.astype(BF16))
    ssd_params = (norm_w, ssd_in_w.astype(BF16), ssd_conv_w, ssd_conv_b, ssd_dt_bias, ssd_A_log,
                  jnp.repeat(ssd_D, HEAD_DIM, axis=1), ssd_norm_w, ssd_out_w.astype(BF16),
                  final_norm_w.reshape(1, D_MODEL))
    pool_p, conv_p = [], []
    pool_s = ssd_s = None
    for i in range(DEPTH):
        j = i // 2
        if i % 2 == 0:
            xp, st = _pool_prompt(xp, pool_params, i, j)
            pool_p.append(st)
            xs, pool_s = _pool_sample(xs, pool_in, pool_params, i, j, pool_s, seq=seq)
        else:
            final = i == DEPTH - 1
            xp, cst, xs, ssd_s = _ssd_layer(xp, xs, state_conv, ssm_in, ssd_params, i, j, ssd_s, seq=seq, final=final)
            conv_p.append(cst)
    ssm_p, conv_s, ssm_s = ssd_s
    return (xp, xs.reshape(nb, seq, D_MODEL), jnp.stack(pool_p), pool_s.transpose(0, 2, 1, 3), jnp.stack(conv_p), conv_s,
            ssm_p.reshape((-1,) + x_prompt.shape[:1] + state_ssm.shape[2:]), ssm_s.reshape(state_ssm.shape))
```

```python
import functools

import jax
import jax.numpy as jnp
from jax import lax
from jax.experimental import pallas as pl
from jax.experimental.pallas import tpu as pltpu

D_MODEL = 1024
DEPTH = 4
PAST_LEN = 16384
D_INNER = 2 * D_MODEL
POOL_WINDOWS = (2, 4, 8, 16)
POOL_GROUP = D_INNER // len(POOL_WINDOWS)
POOL_BUF = max(POOL_WINDOWS) - 1
HEAD_DIM = 64
N_HEADS = D_INNER // HEAD_DIM
D_STATE = 128
N_GROUPS = 4
HEADS_PER_GROUP = N_HEADS // N_GROUPS
GROUP_W = HEADS_PER_GROUP * HEAD_DIM
CONV_K = 4
CONV_DIM = D_INNER + 2 * N_GROUPS * D_STATE
CHUNK = 128
EPS = 1e-6

F32 = jnp.float32
BF16 = jnp.bfloat16

SUBLANES = 8
LANES = 128
SLAB = 512
VMEM_LIMIT_BYTES = 60 * 1024 * 1024

N_WIN = 2 * D_INNER // SLAB
N_WOUT = D_MODEL // SLAB
N_WZ = D_INNER // SLAB
N_WXBC = CONV_DIM // SLAB
DT_LANE_BLOCK = (D_INNER + CONV_DIM) // LANES

POOL_TL = 512
SSD_TL = 256
POOL_BT = 32


def _const_spec(shape):
    zeros = (0,) * len(shape)
    return pl.BlockSpec(shape, lambda *_: zeros, pipeline_mode=pl.Buffered(1))


def _layer_spec(shape, layer):
    idx = (layer,) + (0,) * len(shape)
    return pl.BlockSpec((None,) + tuple(shape), lambda *_: idx, pipeline_mode=pl.Buffered(1))


def _slab_specs(rows, layer, first, count):
    def spec(c):
        return pl.BlockSpec((None, rows, SLAB), lambda *_: (layer, 0, c), pipeline_mode=pl.Buffered(1))
    return [spec(first + c) for c in range(count)]


def _split(refs, *counts):
    out, pos = [], 0
    for n in counts:
        if n is None:
            out.append(refs[pos])
            pos += 1
        else:
            out.append(refs[pos:pos + n])
            pos += n
    assert pos == len(refs)
    return out


def _wcols(slabs, c0, width, krows=slice(None)):
    s0, off = divmod(c0, SLAB)
    if width <= SLAB:
        return slabs[s0][krows, off:off + width]
    return jnp.concatenate([slabs[s0 + i][krows, :] for i in range(width // SLAB)], axis=1)


def _rms(x, w):
    return x * lax.rsqrt(jnp.mean(x * x, axis=-1, keepdims=True) + EPS) * w


def _silu(x):
    half = 0.5 * x
    return half + half * jnp.tanh(half)


def _softplus(x):
    return jnp.maximum(x, 0.0) + jnp.log1p(jnp.exp(-jnp.abs(x)))


def _split3(v):
    v1 = v.astype(BF16).astype(F32)
    r1 = v - v1
    v2 = r1.astype(BF16).astype(F32)
    v3 = (r1 - v2).astype(BF16).astype(F32)
    return v1, v2, v3


def _sum3(f, v):
    p1, p2, p3 = (f(p.astype(BF16)) for p in _split3(v))
    return p1 + p2 + p3


def _dot01(m01, v):
    return _sum3(lambda p: jnp.dot(m01, p, preferred_element_type=F32), v)


def _dot01_tn(v, m01):
    dn = (((0,), (0,)), ((), ()))
    return _sum3(lambda p: lax.dot_general(p, m01, dn, preferred_element_type=F32), v)


def _dot01_r(v, m01):
    return _sum3(lambda p: jnp.dot(p, m01, preferred_element_type=F32), v)


def _expand_heads(v, r3):
    parts = jnp.concatenate(_split3(v), axis=1).astype(BF16)
    return jnp.dot(parts, r3, preferred_element_type=F32)


_POOL_WEIGHT_COUNTS = (None, N_WIN, None, None, N_WOUT)


def _pool_weight_operands(params, i, j):
    norm_w, win, wmix, scale, wout = params
    operands = [norm_w] + [win] * N_WIN + [wmix, scale] + [wout] * N_WOUT
    specs = ([_const_spec(norm_w.shape)] + _slab_specs(D_MODEL, j, 0, N_WIN)
             + [_layer_spec(wmix.shape[1:], j), _const_spec(scale.shape)] + _slab_specs(D_INNER, j, 0, N_WOUT))
    return operands, specs


def _reorder_rows(perm, v):
    return jnp.dot(perm, v, preferred_element_type=F32).astype(BF16)


def _pool_tail(p_of_group, z_of_group, wmix_ref, scale, wout, unperm=None):
    acc = None
    for g in range(len(POOL_WINDOWS)):
        cols = slice(g * POOL_GROUP, (g + 1) * POOL_GROUP)
        mixed = jnp.dot(p_of_group(g).astype(BF16), wmix_ref[g], preferred_element_type=F32)
        y = (mixed * scale[:, cols] * _silu(z_of_group(g))).astype(BF16)
        if unperm is not None:
            y = _reorder_rows(unperm, y)
        part = jnp.dot(y, _wcols(wout, 0, D_MODEL, cols), preferred_element_type=F32)
        acc = part if acc is None else acc + part
    return acc


def _pool_prompt_kernel(*refs, tl, nt, i, j):
    x_ref, nw_ref, win, wmix_ref, scale_ref, wout, xo_ref, st_ref, ext_ref = _split(
        refs, None, *_POOL_WEIGHT_COUNTS, None, None, None)
    t = pl.program_id(1)
    hist = 2 * SUBLANES
    x = x_ref[0]

    @pl.when(t == 0)
    def _():
        ext_ref[0:hist, :] = jnp.zeros((hist, D_INNER), F32)

    half = tl // 2
    hs = []
    for r0 in (0, half):
        hs.append(_rms(x_ref[0, r0:r0 + half, :], nw_ref[i:i + 1, :]).astype(BF16))
        ext_ref[hist + r0:hist + r0 + half, :] = jnp.dot(hs[-1], _wcols(win, 0, D_INNER), preferred_element_type=F32)
    h = jnp.concatenate(hs, axis=0)
    pos = t * tl + lax.broadcasted_iota(jnp.int32, (tl, 1), 0)

    def p_of_group(g):
        w = POOL_WINDOWS[g]
        cols = slice(g * POOL_GROUP, (g + 1) * POOL_GROUP)
        u = ext_ref[hist:hist + tl, cols]
        s = u
        for k in range(1, w):
            s = s + ext_ref[hist - k:hist - k + tl, cols]
        cnt = jnp.minimum(pos + 1, w).astype(F32)
        return s * (1.0 / cnt) - u

    def z_of_group(g):
        return jnp.dot(h, _wcols(win, D_INNER + g * POOL_GROUP, POOL_GROUP), preferred_element_type=F32)

    xo_ref[0] = x + _pool_tail(p_of_group, z_of_group, wmix_ref, scale_ref[j:j + 1, :], wout)

    @pl.when(t == nt - 1)
    def _():
        st_ref[0] = ext_ref[hist + tl - POOL_BUF:hist + tl, :]

    ext_ref[0:hist, :] = ext_ref[tl:tl + hist, :]


def _pool_prompt(x, params, i, j):
    b, l, _ = x.shape
    tl = POOL_TL
    nt = l // tl
    w_operands, w_specs = _pool_weight_operands(params, i, j)
    return pl.pallas_call(
        functools.partial(_pool_prompt_kernel, tl=tl, nt=nt, i=i, j=j),
        grid=(b, nt),
        in_specs=[pl.BlockSpec((1, tl, D_MODEL), lambda s, t: (s, t, 0))] + w_specs,
        out_specs=[
            pl.BlockSpec((1, tl, D_MODEL), lambda s, t: (s, t, 0)),
            pl.BlockSpec((1, POOL_BUF, D_INNER), lambda s, t: (s, 0, 0)),
        ],
        out_shape=[
            jax.ShapeDtypeStruct((b, l, D_MODEL), F32),
            jax.ShapeDtypeStruct((b, POOL_BUF, D_INNER), F32),
        ],
        scratch_shapes=[pltpu.VMEM((tl + 2 * SUBLANES, D_INNER), F32)],
        compiler_params=pltpu.CompilerParams(
            dimension_semantics=("arbitrary", "arbitrary"), vmem_limit_bytes=VMEM_LIMIT_BYTES),
        name="pool_prompt",
    )(x, *w_operands)


def _pool_sample_kernel(*refs, bt, seq, start, i, j, n_prev):
    x_ref, buf_ref, perm_ref, unperm_ref, nw_ref, win, wmix_ref, scale_ref, wout, _, xo_ref, st_ref, u_ref = _split(
        refs, None, None, None, None, *_POOL_WEIGHT_COUNTS, n_prev, None, None, None)
    x = x_ref[...]
    h = _reorder_rows(perm_ref[...], _rms(x, nw_ref[i:i + 1, :]).astype(BF16))
    u_ref[...] = jnp.dot(h, _wcols(win, 0, D_INNER), preferred_element_type=F32).reshape(seq, bt, D_INNER)

    def src(t, cols):
        return u_ref[t, :, cols] if t >= 0 else buf_ref[0, POOL_BUF + t, :, cols]

    def p_of_group(g):
        w = POOL_WINDOWS[g]
        cols = slice(g * POOL_GROUP, (g + 1) * POOL_GROUP)
        outs = []
        for t in range(seq):
            cur = src(t, cols)
            s = cur
            for k in range(1, w):
                s = s + src(t - k, cols)
            outs.append(s * (1.0 / min(start + t + 1, w)) - cur)
        return jnp.concatenate(outs, axis=0)

    def z_of_group(g):
        return jnp.dot(h, _wcols(win, D_INNER + g * POOL_GROUP, POOL_GROUP), preferred_element_type=F32)

    xo_ref[...] = x + _pool_tail(p_of_group, z_of_group, wmix_ref, scale_ref[j:j + 1, :], wout, unperm_ref[...])
    keep = POOL_BUF - seq
    st_ref[0, 0:keep] = buf_ref[0, seq:POOL_BUF]
    st_ref[0, keep:POOL_BUF] = u_ref[...]


def _token_major_perm(bt, seq):
    r = jnp.arange(bt * seq)
    perm = (r[None, :] == ((r % bt) * seq + r // bt)[:, None]).astype(BF16)
    return perm, perm.T


def _pool_sample(x, state, params, i, j, prev, *, seq):
    rows = x.shape[0]
    nb = rows // seq
    bt = POOL_BT
    w_operands, w_specs = _pool_weight_operands(params, i, j)
    perms = _token_major_perm(bt, seq)
    operands = [x, state, *perms] + w_operands
    prev = [] if prev is None else [prev]
    aliases = {len(operands) + k: 1 + k for k in range(len(prev))}
    state_spec = pl.BlockSpec((1, POOL_BUF, bt, D_INNER), lambda s: (j, 0, s, 0))
    return pl.pallas_call(
        functools.partial(_pool_sample_kernel, bt=bt, seq=seq, start=PAST_LEN, i=i, j=j, n_prev=len(prev)),
        grid=(nb // bt,),
        in_specs=[pl.BlockSpec((bt * seq, D_MODEL), lambda s: (s, 0)), state_spec]
        + [_const_spec(p.shape) for p in perms] + w_specs + [pl.BlockSpec(memory_space=pl.ANY)] * len(prev),
        out_specs=[pl.BlockSpec((bt * seq, D_MODEL), lambda s: (s, 0)), state_spec],
        out_shape=[jax.ShapeDtypeStruct((rows, D_MODEL), F32), jax.ShapeDtypeStruct(state.shape, F32)],
        input_output_aliases=aliases,
        scratch_shapes=[pltpu.VMEM((seq, bt, D_INNER), F32)],
        compiler_params=pltpu.CompilerParams(
            dimension_semantics=("arbitrary",), vmem_limit_bytes=VMEM_LIMIT_BYTES),
        name="pool_sample",
    )(*operands, *prev)


_SSD_PARAM_COUNTS = (None, N_WZ, N_WXBC) + (None,) * 7 + (N_WOUT,) + (None,) * 2
N_MASKS = 4


def _ssd_block_masks(q):
    i = jnp.arange(CHUNK)[:, None]
    j = jnp.arange(CHUNK)[None, :]
    same = (i // q) == (j // q)
    tril = (same & (j <= i)).astype(BF16)
    ones = same.astype(BF16)
    eye = (i == j).astype(BF16)
    return [tril, tril.T, ones, eye]


def _head_expander():
    h = jnp.arange(N_HEADS)[:, None]
    c = jnp.arange(D_INNER)[None, :] // HEAD_DIM
    r = (h == c).astype(BF16)
    return jnp.concatenate([r, r, r], axis=0)


def _ssd_param_operands(params, j):
    norm_w, w_in, conv_w, conv_b, dt_bias, a_log, d_cols, ssd_norm_w, w_out, final_norm_w = params
    small = [conv_w, conv_b, dt_bias, a_log, d_cols, ssd_norm_w]
    tail = [final_norm_w, _head_expander()]
    operands = [norm_w] + [w_in] * (N_WZ + N_WXBC + 1) + small + [w_out] * N_WOUT + tail
    dt_spec = pl.BlockSpec((None, D_MODEL, LANES), lambda *_: (j, 0, DT_LANE_BLOCK), pipeline_mode=pl.Buffered(1))
    specs = ([_const_spec(norm_w.shape)] + _slab_specs(D_MODEL, j, 0, N_WZ) + _slab_specs(D_MODEL, j, N_WZ, N_WXBC)
             + [dt_spec] + [_const_spec(a.shape) for a in small] + _slab_specs(D_INNER, j, 0, N_WOUT)
             + [_const_spec(a.shape) for a in tail])
    return operands, specs


def _dt_and_decay(dt_raw, dtb_ref, alog_ref, j):
    dt = _softplus(dt_raw + dtb_ref[j:j + 1, :])
    return dt, dt * (-jnp.exp(alog_ref[j:j + 1, :]))


def _ssd_block_sums(a_c, dt_c, tril, triu, ones, eye):
    acum = _dot01(tril, a_c)
    acum_t = _dot01_tn(a_c, triu)
    alast = _dot01(ones, a_c)
    dt_t = _dot01_tn(dt_c, eye)
    return acum, acum_t, alast, dt_t


def _ssd_block_diag(r0, sums, act_ref, ybuf_ref, q, side_work=None, carried=None):
    acum, acum_t, _, dt_t = sums
    ii = lax.broadcasted_iota(jnp.int32, (CHUNK, CHUNK), 0)
    jj = lax.broadcasted_iota(jnp.int32, (CHUNK, CHUNK), 1)
    causal = jj <= ii
    if q != CHUNK:
        causal = causal & ((ii // q) == (jj // q))
    rows = slice(r0, r0 + CHUNK)
    first_of_pair = lax.broadcasted_iota(jnp.int32, (CHUNK, 2 * HEAD_DIM), 1) < HEAD_DIM
    for g in range(N_GROUPS):
        if side_work is not None:
            side_work(g)
        b0 = D_INNER + g * D_STATE
        c0 = D_INNER + N_GROUPS * D_STATE + g * D_STATE
        bg = act_ref[rows, b0:b0 + D_STATE].astype(BF16)
        cg = act_ref[rows, c0:c0 + D_STATE].astype(BF16)
        cb = lax.dot_general(cg, bg, (((1,), (1,)), ((), ())), preferred_element_type=F32)
        y_carried = None if carried is None else carried(g, bg, cg)
        for pair in range(HEADS_PER_GROUP // 2):
            h0 = g * HEADS_PER_GROUP + 2 * pair
            ws = []
            for hh in (h0, h0 + 1):
                seg = acum[:, hh:hh + 1] - acum_t[hh:hh + 1, :]
                decay = jnp.exp(jnp.where(causal, seg, -jnp.inf))
                ws.append((cb * decay * dt_t[hh:hh + 1, :]).astype(BF16))
            pcols = slice(h0 * HEAD_DIM, (h0 + 2) * HEAD_DIM)
            xp = act_ref[rows, pcols]
            rhs = jnp.concatenate([jnp.where(first_of_pair, xp, 0.0), jnp.where(first_of_pair, 0.0, xp)], axis=0)
            yd = jnp.dot(jnp.concatenate(ws, axis=1), rhs.astype(BF16), preferred_element_type=F32)
            if y_carried is not None:
                yd = yd + y_carried[:, 2 * pair * HEAD_DIM:2 * (pair + 1) * HEAD_DIM]
            ybuf_ref[rows, pcols] = yd


def _ssd_epilogue(x, gate, act_ref, ybuf_ref, d_cols, norm_w, wout, fnw_ref, final):
    y = ybuf_ref[...] + d_cols * act_ref[:, :D_INNER]
    y = _rms(y * gate, norm_w)
    out = x + jnp.dot(y.astype(BF16), _wcols(wout, 0, D_MODEL), preferred_element_type=F32)
    return _rms(out, fnw_ref[...]) if final else out


def _causal_conv(ext, cw_ref, cb_ref, j, cols=slice(None)):
    out = ext(0) * cw_ref[j, 0:1, cols]
    for k in range(1, CONV_K):
        out = out + ext(k) * cw_ref[j, k:k + 1, cols]
    return _silu(out + cb_ref[j:j + 1, cols])


def _ssd_prompt_phases(x_ref, params, masks, outs, scratch, *, tl, i, j, final):
    nw_ref, wz, wxbc, wdt_ref, cw_ref, cb_ref, dtb_ref, alog_ref, dx_ref, normw_ref, wout, fnw_ref, r3_ref = params
    tril_ref, triu_ref, ones_ref, eye_ref = masks
    xo_ref, cst_ref, ssm_ref = outs
    cext_ref, act_ref, ybuf_ref, ht_ref, gate_ref = scratch

    def first():
        cext_ref[0:SUBLANES, :] = jnp.zeros((SUBLANES, CONV_DIM), F32)
        ht_ref[...] = jnp.zeros_like(ht_ref)

    def last():
        cst_ref[0] = cext_ref[SUBLANES - (CONV_K - 1):SUBLANES, :]
        ssm_ref[0, 0] = ht_ref[...].T

    def main():
        _ssd_prompt_main(x_ref, nw_ref, wz, wxbc, wdt_ref, cw_ref, cb_ref, dtb_ref, alog_ref, dx_ref, normw_ref, wout,
                         fnw_ref, r3_ref, tril_ref, triu_ref, ones_ref, eye_ref, xo_ref, cext_ref, act_ref, ybuf_ref,
                         ht_ref, gate_ref, tl=tl, i=i, j=j, final=final)

    return first, main, last


def _ssd_prompt_main(x_ref, nw_ref, wz, wxbc, wdt_ref, cw_ref, cb_ref, dtb_ref, alog_ref, dx_ref, normw_ref, wout,
                     fnw_ref, r3_ref, tril_ref, triu_ref, ones_ref, eye_ref, xo_ref, cext_ref, act_ref, ybuf_ref,
                     ht_ref, gate_ref, *, tl, i, j, final):
    h = _rms(x_ref[0], nw_ref[i:i + 1, :]).astype(BF16)
    dt, a = _dt_and_decay(jnp.dot(h, wdt_ref[:, 0:N_HEADS], preferred_element_type=F32), dtb_ref, alog_ref, j)
    lo = SUBLANES - (CONV_K - 1)
    for sl in range(N_WXBC):
        cols = slice(sl * SLAB, (sl + 1) * SLAB)
        cext_ref[SUBLANES:SUBLANES + tl, cols] = jnp.dot(h, wxbc[sl][...], preferred_element_type=F32)
        act_ref[:, cols] = _causal_conv(lambda k: cext_ref[lo + k:lo + k + tl, cols], cw_ref, cb_ref, j, cols)

    tril, triu, ones, eye = tril_ref[...], triu_ref[...], ones_ref[...], eye_ref[...]
    n_chunks = tl // CHUNK
    zw = D_INNER // (n_chunks * N_GROUPS)
    for c in range(n_chunks):
        r0 = c * CHUNK
        rows = slice(r0, r0 + CHUNK)
        a_c, dt_c = a[rows], dt[rows]

        def gate_piece(g, c=c):
            z0 = (c * N_GROUPS + g) * zw
            gate_ref[:, z0:z0 + zw] = _silu(jnp.dot(h, _wcols(wz, z0, zw), preferred_element_type=F32))

        sums = _ssd_block_sums(a_c, dt_c, tril, triu, ones, eye)
        acum, _, alast, _ = sums
        scales = jnp.concatenate(
            [jnp.exp(acum), jnp.exp(alast - acum) * dt_c, jnp.exp(alast[0:SUBLANES])], axis=0)
        scales_x = _expand_heads(scales, r3_ref[...])

        def carried(g, bg, cg, rows=rows, scales_x=scales_x):
            gcols = slice(g * GROUP_W, (g + 1) * GROUP_W)
            hprev = ht_ref[:, gcols]
            y_off = jnp.dot(cg, hprev.astype(BF16), preferred_element_type=F32) * scales_x[0:CHUNK, gcols]
            xs = (act_ref[rows, gcols] * scales_x[CHUNK:2 * CHUNK, gcols]).astype(BF16)
            st = lax.dot_general(bg, xs, (((0,), (0,)), ((), ())), preferred_element_type=F32)
            ht_ref[:, gcols] = hprev * scales_x[2 * CHUNK:2 * CHUNK + 1, gcols] + st
            return y_off

        _ssd_block_diag(r0, sums, act_ref, ybuf_ref, CHUNK, gate_piece, carried)

    xo_ref[0] = _ssd_epilogue(x_ref[0], gate_ref[...], act_ref, ybuf_ref, dx_ref[j:j + 1, :], normw_ref[j:j + 1, :],
                              wout, fnw_ref, final)
    cext_ref[0:SUBLANES, :] = cext_ref[tl:tl + SUBLANES, :]


def _ssd_sample_phases(x_ref, cin_ref, hin_ref, params, masks, outs, scratch, *, bt, sb, seq, i, j, final):
    nw_ref, wz, wxbc, wdt_ref, cw_ref, cb_ref, dtb_ref, alog_ref, dx_ref, normw_ref, wout, fnw_ref, r3_ref = params
    tril_ref, triu_ref, ones_ref, eye_ref = masks
    xo_ref, cst_ref, hout_ref = outs
    cext_ref, act_ref, ybuf_ref, eax_ref, xsc_ref, cdt_ref, h_ref = scratch
    s_idx = pl.program_id(1)
    rows_total = bt * seq

    def first():
        h = _rms(x_ref[...], nw_ref[i:i + 1, :]).astype(BF16)
        h_ref[...] = h
        xbc = jnp.dot(h, _wcols(wxbc, 0, CONV_DIM), preferred_element_type=F32)
        dt, a = _dt_and_decay(jnp.dot(h, wdt_ref[:, 0:N_HEADS], preferred_element_type=F32), dtb_ref, alog_ref, j)
        lo = SUBLANES - (CONV_K - 1)
        cext_ref[:, 0:SUBLANES, :] = jnp.zeros((bt, SUBLANES, CONV_DIM), F32)
        for k in range(CONV_K - 1):
            cext_ref[:, lo + k, :] = cin_ref[0, k]
        cext_ref[:, SUBLANES:SUBLANES + seq, :] = xbc.reshape(bt, seq, CONV_DIM)
        conv = _causal_conv(lambda k: cext_ref[:, lo + k:lo + k + seq, :], cw_ref, cb_ref, j)
        act_ref[...] = conv.reshape(rows_total, CONV_DIM)
        for k in range(CONV_K - 1):
            cst_ref[0, k] = cext_ref[:, SUBLANES + seq - (CONV_K - 1) + k, :]
        tril, triu, ones, eye = tril_ref[...], triu_ref[...], ones_ref[...], eye_ref[...]
        for c in range(rows_total // CHUNK):
            r0 = c * CHUNK
            rows = slice(r0, r0 + CHUNK)
            a_c, dt_c = a[rows], dt[rows]
            sums = _ssd_block_sums(a_c, dt_c, tril, triu, ones, eye)
            acum, _, alast, _ = sums
            _ssd_block_diag(r0, sums, act_ref, ybuf_ref, seq)
            scales = jnp.concatenate([jnp.exp(acum), jnp.exp(alast - acum) * dt_c], axis=0)
            scales_x = _expand_heads(scales, r3_ref[...])
            eax_ref[rows, :] = scales_x[0:CHUNK]
            xsc_ref[rows, :] = act_ref[rows, 0:D_INNER] * scales_x[CHUNK:2 * CHUNK]
            cdt_ref[:, rows] = jnp.exp(_dot01_tn(a_c, ones))

    def main():
        for si in range(sb):
            _ssd_sample_sequence(si, pl.multiple_of((s_idx * sb + si) * seq, seq))

    def _ssd_sample_sequence(si, r0):
        rows = pl.ds(r0, seq)
        pick = (lax.broadcasted_iota(jnp.int32, (rows_total, D_STATE), 0) == r0).astype(BF16)
        cd = _dot01_r(cdt_ref[...], pick)
        for g in range(N_GROUPS):
            gcols = slice(g * GROUP_W, (g + 1) * GROUP_W)
            b0 = D_INNER + g * D_STATE
            c0 = D_INNER + N_GROUPS * D_STATE + g * D_STATE
            bg = act_ref[rows, b0:b0 + D_STATE].astype(BF16)
            cg = act_ref[rows, c0:c0 + D_STATE].astype(BF16)
            hprev = hin_ref[0, si, g * GROUP_W:(g + 1) * GROUP_W, :]
            y_off = lax.dot_general(cg, hprev.astype(BF16), (((1,), (1,)), ((), ())), preferred_element_type=F32)
            ybuf_ref[rows, gcols] = ybuf_ref[rows, gcols] + y_off * eax_ref[rows, gcols]
            st = lax.dot_general(xsc_ref[rows, gcols].astype(BF16), bg, (((0,), (0,)), ((), ())),
                                 preferred_element_type=F32)
            for e in range(HEADS_PER_GROUP):
                hh = g * HEADS_PER_GROUP + e
                hrows = slice(hh * HEAD_DIM, (hh + 1) * HEAD_DIM)
                hout_ref[0, si, hrows, :] = (hin_ref[0, si, hrows, :] * cd[hh:hh + 1, :]
                                             + st[e * HEAD_DIM:(e + 1) * HEAD_DIM, :])

    def last():
        gate = _silu(jnp.dot(h_ref[...], _wcols(wz, 0, D_INNER), preferred_element_type=F32))
        xo_ref[...] = _ssd_epilogue(x_ref[...], gate, act_ref, ybuf_ref, dx_ref[j:j + 1, :], normw_ref[j:j + 1, :],
                                    wout, fnw_ref, final)

    return first, main, last


def _ssd_kernel(*refs, tl, nt, bt, sb, seq, i, j, final, n_prev):
    (xp_ref, xs_ref, cin_ref, hin_ref, *params, masks_p, masks_s, _, outs_p, outs_s, scratch_p, scratch_s) = _split(
        refs, None, None, None, None, *_SSD_PARAM_COUNTS, N_MASKS, N_MASKS, n_prev, 3, 3, 5, 7)
    prompt = _ssd_prompt_phases(xp_ref, params, masks_p, outs_p, scratch_p, tl=tl, i=i, j=j, final=final)
    sample = _ssd_sample_phases(xs_ref, cin_ref, hin_ref, params, masks_s, outs_s, scratch_s, bt=bt, sb=sb, seq=seq,
                                i=i, j=j, final=final)
    t = pl.program_id(1)
    for when, phase in ((t == 0, 0), (None, 1), (t == nt - 1, 2)):
        def run(phase=phase):
            prompt[phase]()
            sample[phase]()
        run() if when is None else pl.when(when)(run)


def _ssd_layer(xp, xs, state_conv, state_ssm, params, i, j, prev, *, seq, final):
    b, l, _ = xp.shape
    tl = SSD_TL
    nt = l // tl
    rows = xs.shape[0]
    nb = rows // seq
    bt = nb // b
    sb = bt // nt
    assert bt * b == nb and sb * nt == bt and (bt * seq) % CHUNK == 0
    hp = N_HEADS * HEAD_DIM
    p_operands, p_specs = _ssd_param_operands(params, j)
    masks = _ssd_block_masks(CHUNK) + _ssd_block_masks(seq)
    operands = [xp, xs, state_conv, state_ssm] + p_operands + masks
    prev = [] if prev is None else list(prev)
    aliases = {len(operands) + k: out for k, out in zip(range(len(prev)), (2, 4, 5))}
    n_layers = state_ssm.shape[0]
    conv_spec = pl.BlockSpec((1, CONV_K - 1, bt, CONV_DIM), lambda s, t: (j, 0, s, 0))
    ssm_spec = pl.BlockSpec((1, sb, hp, D_STATE), lambda s, t: (j, s * nt + t, 0, 0))
    xo, conv_p, ssm_p, xso, conv_s, ssm_s = pl.pallas_call(
        functools.partial(_ssd_kernel, tl=tl, nt=nt, bt=bt, sb=sb, seq=seq, i=i, j=j, final=final, n_prev=len(prev)),
        grid=(b, nt),
        in_specs=[
            pl.BlockSpec((1, tl, D_MODEL), lambda s, t: (s, t, 0)),
            pl.BlockSpec((bt * seq, D_MODEL), lambda s, t: (s, 0)),
            conv_spec,
            ssm_spec,
        ] + p_specs + [_const_spec(m.shape) for m in masks] + [pl.BlockSpec(memory_space=pl.ANY)] * len(prev),
        out_specs=[
            pl.BlockSpec((1, tl, D_MODEL), lambda s, t: (s, t, 0)),
            pl.BlockSpec((1, CONV_K - 1, CONV_DIM), lambda s, t: (s, 0, 0)),
            pl.BlockSpec((1, 1, hp, D_STATE), lambda s, t: (j, s, 0, 0)),
            pl.BlockSpec((bt * seq, D_MODEL), lambda s, t: (s, 0)),
            conv_spec,
            ssm_spec,
        ],
        out_shape=[
            jax.ShapeDtypeStruct((b, l, D_MODEL), F32),
            jax.ShapeDtypeStruct((b, CONV_K - 1, CONV_DIM), F32),
            jax.ShapeDtypeStruct((n_layers, b, hp, D_STATE), F32),
            jax.ShapeDtypeStruct((rows, D_MODEL), F32),
            jax.ShapeDtypeStruct((n_layers, CONV_K - 1, nb, CONV_DIM), F32),
            jax.ShapeDtypeStruct((n_layers, nb, hp, D_STATE), F32),
        ],
        input_output_aliases=aliases,
        scratch_shapes=[
            pltpu.VMEM((tl + SUBLANES, CONV_DIM), F32),
            pltpu.VMEM((tl, CONV_DIM), F32),
            pltpu.VMEM((tl, D_INNER), F32),
            pltpu.VMEM((D_STATE, hp), F32),
            pltpu.VMEM((tl, D_INNER), F32),
            pltpu.VMEM((bt, SUBLANES + seq, CONV_DIM), F32),
            pltpu.VMEM((bt * seq, CONV_DIM), F32),
            pltpu.VMEM((bt * seq, D_INNER), F32),
            pltpu.VMEM((bt * seq, D_INNER), F32),
            pltpu.VMEM((bt * seq, D_INNER), F32),
            pltpu.VMEM((N_HEADS, bt * seq), F32),
            pltpu.VMEM((bt * seq, D_MODEL), BF16),
        ],
        compiler_params=pltpu.CompilerParams(
            dimension_semantics=("arbitrary", "arbitrary"), vmem_limit_bytes=VMEM_LIMIT_BYTES),
        name="ssd_layer",
    )(*operands, *prev)
    return xo, conv_p, xso, (ssm_p, conv_s, ssm_s)


def kernel(x_prompt, x_sample, state_pool, state_conv, state_ssm, norm_w, pool_in_w, pool_mix_w, pool_scale,
           pool_out_w, ssd_in_w, ssd_conv_w, ssd_conv_b, ssd_dt_bias, ssd_A_log, ssd_D, ssd_norm_w, ssd_out_w,
           final_norm_w):
    nb, seq, _ = x_sample.shape
    hp = N_HEADS * HEAD_DIM
    xp = x_prompt
    xs = x_sample.reshape(nb * seq, D_MODEL)
    ssm_in = state_ssm.reshape(state_ssm.shape[0], nb, hp, D_STATE)
    pool_in = state_pool.transpose(0, 2, 1, 3)
    conv_in = state_conv.transpose(0, 2, 1, 3)
    pool_params = (norm_w, pool_in_w.astype(BF16), pool_mix_w.astype(BF16), pool_scale, pool_out_w.astype(BF16))
    ssd_params = (norm_w, ssd_in_w.astype(BF16), ssd_conv_w, ssd_conv_b, ssd_dt_bias, ssd_A_log,
                  jnp.repeat(ssd_D, HEAD_DIM, axis=1), ssd_norm_w, ssd_out_w.astype(BF16),
                  final_norm_w.reshape(1, D_MODEL))
    pool_p, conv_p = [], []
    pool_s = ssd_s = None
    for i in range(DEPTH):
        j = i // 2
        if i % 2 == 0:
            xp, st = _pool_prompt(xp, pool_params, i, j)
            pool_p.append(st)
            xs, pool_s = _pool_sample(xs, pool_in, pool_params, i, j, pool_s, seq=seq)
        else:
            final = i == DEPTH - 1
            xp, cst, xs, ssd_s = _ssd_layer(xp, xs, conv_in, ssm_in, ssd_params, i, j, ssd_s, seq=seq, final=final)
            conv_p.append(cst)
    ssm_p, conv_s, ssm_s = ssd_s
    return (xp, xs.reshape(nb, seq, D_MODEL), jnp.stack(pool_p), pool_s.transpose(0, 2, 1, 3), jnp.stack(conv_p),
            conv_s.transpose(0, 2, 1, 3),
            ssm_p.reshape((-1,) + x_prompt.shape[:1] + state_ssm.shape[2:]), ssm_s.reshape(state_ssm.shape))
```

```python
import functools

import jax
import jax.numpy as jnp
from jax import lax
from jax.experimental import pallas as pl
from jax.experimental.pallas import tpu as pltpu

D_MODEL = 1024
DEPTH = 4
PAST_LEN = 16384
D_INNER = 2 * D_MODEL
POOL_WINDOWS = (2, 4, 8, 16)
POOL_GROUP = D_INNER // len(POOL_WINDOWS)
POOL_BUF = max(POOL_WINDOWS) - 1
HEAD_DIM = 64
N_HEADS = D_INNER // HEAD_DIM
D_STATE = 128
N_GROUPS = 4
HEADS_PER_GROUP = N_HEADS // N_GROUPS
GROUP_W = HEADS_PER_GROUP * HEAD_DIM
CONV_K = 4
CONV_DIM = D_INNER + 2 * N_GROUPS * D_STATE
CHUNK = 128
EPS = 1e-6

F32 = jnp.float32
BF16 = jnp.bfloat16

SUBLANES = 8
LANES = 128
SLAB = 512
VMEM_LIMIT_BYTES = 60 * 1024 * 1024

N_WIN = 2 * D_INNER // SLAB
N_WOUT = D_MODEL // SLAB
N_WZ = D_INNER // SLAB
N_WXBC = CONV_DIM // SLAB
DT_LANE_BLOCK = (D_INNER + CONV_DIM) // LANES

POOL_TL = 512
SSD_TL = 256
POOL_BT = 32


def _const_spec(shape):
    zeros = (0,) * len(shape)
    return pl.BlockSpec(shape, lambda *_: zeros, pipeline_mode=pl.Buffered(1))


def _layer_spec(shape, layer):
    idx = (layer,) + (0,) * len(shape)
    return pl.BlockSpec((None,) + tuple(shape), lambda *_: idx, pipeline_mode=pl.Buffered(1))


def _slab_specs(rows, layer, first, count):
    def spec(c):
        return pl.BlockSpec((None, rows, SLAB), lambda *_: (layer, 0, c), pipeline_mode=pl.Buffered(1))
    return [spec(first + c) for c in range(count)]


def _split(refs, *counts):
    out, pos = [], 0
    for n in counts:
        if n is None:
            out.append(refs[pos])
            pos += 1
        else:
            out.append(refs[pos:pos + n])
            pos += n
    assert pos == len(refs)
    return out


def _wcols(slabs, c0, width, krows=slice(None)):
    s0, off = divmod(c0, SLAB)
    if width <= SLAB:
        return slabs[s0][krows, off:off + width]
    return jnp.concatenate([slabs[s0 + i][krows, :] for i in range(width // SLAB)], axis=1)


def _rms(x, w):
    return x * lax.rsqrt(jnp.mean(x * x, axis=-1, keepdims=True) + EPS) * w


def _silu(x):
    half = 0.5 * x
    return half + half * jnp.tanh(half)


def _softplus(x):
    return jnp.maximum(x, 0.0) + jnp.log1p(jnp.exp(-jnp.abs(x)))


def _split3(v):
    v1 = v.astype(BF16).astype(F32)
    r1 = v - v1
    v2 = r1.astype(BF16).astype(F32)
    v3 = (r1 - v2).astype(BF16).astype(F32)
    return v1, v2, v3


def _sum3(f, v):
    p1, p2, p3 = (f(p.astype(BF16)) for p in _split3(v))
    return p1 + p2 + p3


def _dot01(m01, v):
    return _sum3(lambda p: jnp.dot(m01, p, preferred_element_type=F32), v)


def _dot01_tn(v, m01):
    dn = (((0,), (0,)), ((), ()))
    return _sum3(lambda p: lax.dot_general(p, m01, dn, preferred_element_type=F32), v)


def _expand_heads(v, r3):
    parts = jnp.concatenate(_split3(v), axis=1).astype(BF16)
    return jnp.dot(parts, r3, preferred_element_type=F32)


_POOL_WEIGHT_COUNTS = (None, N_WIN, None, None, N_WOUT)


def _pool_weight_operands(params, i, j):
    norm_w, win, wmix, scale, wout = params
    operands = [norm_w] + [win] * N_WIN + [wmix, scale] + [wout] * N_WOUT
    specs = ([_const_spec(norm_w.shape)] + _slab_specs(D_MODEL, j, 0, N_WIN)
             + [_layer_spec(wmix.shape[1:], j), _const_spec(scale.shape)] + _slab_specs(D_INNER, j, 0, N_WOUT))
    return operands, specs


def _reorder_rows(perm, v):
    return jnp.dot(perm, v, preferred_element_type=F32).astype(BF16)


def _pool_tail(p_of_group, z_of_group, wmix_ref, scale, wout, unperm=None):
    acc = None
    for g in range(len(POOL_WINDOWS)):
        cols = slice(g * POOL_GROUP, (g + 1) * POOL_GROUP)
        mixed = jnp.dot(p_of_group(g).astype(BF16), wmix_ref[g], preferred_element_type=F32)
        y = (mixed * scale[:, cols] * _silu(z_of_group(g))).astype(BF16)
        if unperm is not None:
            y = _reorder_rows(unperm, y)
        part = jnp.dot(y, _wcols(wout, 0, D_MODEL, cols), preferred_element_type=F32)
        acc = part if acc is None else acc + part
    return acc


def _pool_prompt_kernel(*refs, tl, nt, i, j):
    x_ref, nw_ref, win, wmix_ref, scale_ref, wout, xo_ref, st_ref, ext_ref = _split(
        refs, None, *_POOL_WEIGHT_COUNTS, None, None, None)
    t = pl.program_id(1)
    hist = 2 * SUBLANES
    x = x_ref[0]

    @pl.when(t == 0)
    def _():
        ext_ref[0:hist, :] = jnp.zeros((hist, D_INNER), F32)

    half = tl // 2
    hs = []
    for r0 in (0, half):
        hs.append(_rms(x_ref[0, r0:r0 + half, :], nw_ref[i:i + 1, :]).astype(BF16))
        ext_ref[hist + r0:hist + r0 + half, :] = jnp.dot(hs[-1], _wcols(win, 0, D_INNER), preferred_element_type=F32)
    h = jnp.concatenate(hs, axis=0)
    pos = t * tl + lax.broadcasted_iota(jnp.int32, (tl, 1), 0)

    def p_of_group(g):
        w = POOL_WINDOWS[g]
        cols = slice(g * POOL_GROUP, (g + 1) * POOL_GROUP)
        u = ext_ref[hist:hist + tl, cols]
        s = u
        for k in range(1, w):
            s = s + ext_ref[hist - k:hist - k + tl, cols]
        cnt = jnp.minimum(pos + 1, w).astype(F32)
        return s * (1.0 / cnt) - u

    def z_of_group(g):
        return jnp.dot(h, _wcols(win, D_INNER + g * POOL_GROUP, POOL_GROUP), preferred_element_type=F32)

    xo_ref[0] = x + _pool_tail(p_of_group, z_of_group, wmix_ref, scale_ref[j:j + 1, :], wout)

    @pl.when(t == nt - 1)
    def _():
        st_ref[0] = ext_ref[hist + tl - POOL_BUF:hist + tl, :]

    ext_ref[0:hist, :] = ext_ref[tl:tl + hist, :]


def _pool_prompt(x, params, i, j):
    b, l, _ = x.shape
    tl = POOL_TL
    nt = l // tl
    w_operands, w_specs = _pool_weight_operands(params, i, j)
    return pl.pallas_call(
        functools.partial(_pool_prompt_kernel, tl=tl, nt=nt, i=i, j=j),
        grid=(b, nt),
        in_specs=[pl.BlockSpec((1, tl, D_MODEL), lambda s, t: (s, t, 0))] + w_specs,
        out_specs=[
            pl.BlockSpec((1, tl, D_MODEL), lambda s, t: (s, t, 0)),
            pl.BlockSpec((1, POOL_BUF, D_INNER), lambda s, t: (s, 0, 0)),
        ],
        out_shape=[
            jax.ShapeDtypeStruct((b, l, D_MODEL), F32),
            jax.ShapeDtypeStruct((b, POOL_BUF, D_INNER), F32),
        ],
        scratch_shapes=[pltpu.VMEM((tl + 2 * SUBLANES, D_INNER), F32)],
        compiler_params=pltpu.CompilerParams(
            dimension_semantics=("arbitrary", "arbitrary"), vmem_limit_bytes=VMEM_LIMIT_BYTES),
        name="pool_prompt",
    )(x, *w_operands)


def _pool_sample_kernel(*refs, bt, seq, start, i, j, n_prev):
    x_ref, buf_ref, perm_ref, unperm_ref, nw_ref, win, wmix_ref, scale_ref, wout, _, xo_ref, st_ref, u_ref = _split(
        refs, None, None, None, None, *_POOL_WEIGHT_COUNTS, n_prev, None, None, None)
    x = x_ref[...]
    h = _reorder_rows(perm_ref[...], _rms(x, nw_ref[i:i + 1, :]).astype(BF16))
    u_ref[...] = jnp.dot(h, _wcols(win, 0, D_INNER), preferred_element_type=F32).reshape(seq, bt, D_INNER)

    def src(t, cols):
        return u_ref[t, :, cols] if t >= 0 else buf_ref[0, POOL_BUF + t, :, cols]

    def p_of_group(g):
        w = POOL_WINDOWS[g]
        cols = slice(g * POOL_GROUP, (g + 1) * POOL_GROUP)
        outs = []
        for t in range(seq):
            cur = src(t, cols)
            s = cur
            for k in range(1, w):
                s = s + src(t - k, cols)
            outs.append(s * (1.0 / min(start + t + 1, w)) - cur)
        return jnp.concatenate(outs, axis=0)

    def z_of_group(g):
        return jnp.dot(h, _wcols(win, D_INNER + g * POOL_GROUP, POOL_GROUP), preferred_element_type=F32)

    xo_ref[...] = x + _pool_tail(p_of_group, z_of_group, wmix_ref, scale_ref[j:j + 1, :], wout, unperm_ref[...])
    keep = POOL_BUF - seq
    st_ref[0, 0:keep] = buf_ref[0, seq:POOL_BUF]
    st_ref[0, keep:POOL_BUF] = u_ref[...]


def _token_major_perm(bt, seq):
    r = jnp.arange(bt * seq)
    perm = (r[None, :] == ((r % bt) * seq + r // bt)[:, None]).astype(BF16)
    return perm, perm.T


def _pool_sample(x, state, params, i, j, prev, *, seq):
    rows = x.shape[0]
    nb = rows // seq
    bt = POOL_BT
    w_operands, w_specs = _pool_weight_operands(params, i, j)
    perms = _token_major_perm(bt, seq)
    operands = [x, state, *perms] + w_operands
    prev = [] if prev is None else [prev]
    aliases = {len(operands) + k: 1 + k for k in range(len(prev))}
    state_spec = pl.BlockSpec((1, POOL_BUF, bt, D_INNER), lambda s: (j, 0, s, 0))
    return pl.pallas_call(
        functools.partial(_pool_sample_kernel, bt=bt, seq=seq, start=PAST_LEN, i=i, j=j, n_prev=len(prev)),
        grid=(nb // bt,),
        in_specs=[pl.BlockSpec((bt * seq, D_MODEL), lambda s: (s, 0)), state_spec]
        + [_const_spec(p.shape) for p in perms] + w_specs + [pl.BlockSpec(memory_space=pl.ANY)] * len(prev),
        out_specs=[pl.BlockSpec((bt * seq, D_MODEL), lambda s: (s, 0)), state_spec],
        out_shape=[jax.ShapeDtypeStruct((rows, D_MODEL), F32), jax.ShapeDtypeStruct(state.shape, F32)],
        input_output_aliases=aliases,
        scratch_shapes=[pltpu.VMEM((seq, bt, D_INNER), F32)],
        compiler_params=pltpu.CompilerParams(
            dimension_semantics=("arbitrary",), vmem_limit_bytes=VMEM_LIMIT_BYTES),
        name="pool_sample",
    )(*operands, *prev)


_SSD_PARAM_COUNTS = (None, N_WZ, N_WXBC) + (None,) * 7 + (N_WOUT,) + (None,) * 2
N_MASKS = 4


def _ssd_block_masks(q):
    i = jnp.arange(CHUNK)[:, None]
    j = jnp.arange(CHUNK)[None, :]
    same = (i // q) == (j // q)
    tril = (same & (j <= i)).astype(BF16)
    ones = same.astype(BF16)
    eye = (i == j).astype(BF16)
    return [tril, tril.T, ones, eye]


def _head_expander():
    h = jnp.arange(N_HEADS)[:, None]
    c = jnp.arange(D_INNER)[None, :] // HEAD_DIM
    r = (h == c).astype(BF16)
    return jnp.concatenate([r, r, r], axis=0)


def _ssd_param_operands(params, j):
    norm_w, w_in, conv_w, conv_b, dt_bias, a_log, d_cols, ssd_norm_w, w_out, final_norm_w = params
    small = [conv_w, conv_b, dt_bias, a_log, d_cols, ssd_norm_w]
    tail = [final_norm_w, _head_expander()]
    operands = [norm_w] + [w_in] * (N_WZ + N_WXBC + 1) + small + [w_out] * N_WOUT + tail
    dt_spec = pl.BlockSpec((None, D_MODEL, LANES), lambda *_: (j, 0, DT_LANE_BLOCK), pipeline_mode=pl.Buffered(1))
    specs = ([_const_spec(norm_w.shape)] + _slab_specs(D_MODEL, j, 0, N_WZ) + _slab_specs(D_MODEL, j, N_WZ, N_WXBC)
             + [dt_spec] + [_const_spec(a.shape) for a in small] + _slab_specs(D_INNER, j, 0, N_WOUT)
             + [_const_spec(a.shape) for a in tail])
    return operands, specs


def _dt_and_decay(dt_raw, dtb_ref, alog_ref, j):
    dt = _softplus(dt_raw + dtb_ref[j:j + 1, :])
    return dt, dt * (-jnp.exp(alog_ref[j:j + 1, :]))


def _ssd_block_sums(a_c, dt_c, tril, triu, ones, eye):
    acum = _dot01(tril, a_c)
    acum_t = _dot01_tn(a_c, triu)
    alast = _dot01(ones, a_c)
    dt_t = _dot01_tn(dt_c, eye)
    return acum, acum_t, alast, dt_t


def _ssd_block_diag(r0, sums, act_ref, ybuf_ref, q, side_work=None, carried=None):
    acum, acum_t, _, dt_t = sums
    ii = lax.broadcasted_iota(jnp.int32, (CHUNK, CHUNK), 0)
    jj = lax.broadcasted_iota(jnp.int32, (CHUNK, CHUNK), 1)
    causal = jj <= ii
    if q != CHUNK:
        causal = causal & ((ii // q) == (jj // q))
    rows = slice(r0, r0 + CHUNK)
    first_of_pair = lax.broadcasted_iota(jnp.int32, (CHUNK, 2 * HEAD_DIM), 1) < HEAD_DIM
    for g in range(N_GROUPS):
        if side_work is not None:
            side_work(g)
        b0 = D_INNER + g * D_STATE
        c0 = D_INNER + N_GROUPS * D_STATE + g * D_STATE
        bg = act_ref[rows, b0:b0 + D_STATE].astype(BF16)
        cg = act_ref[rows, c0:c0 + D_STATE].astype(BF16)
        cb = lax.dot_general(cg, bg, (((1,), (1,)), ((), ())), preferred_element_type=F32)
        y_carried = None if carried is None else carried(g, bg, cg)
        for pair in range(HEADS_PER_GROUP // 2):
            h0 = g * HEADS_PER_GROUP + 2 * pair
            ws = []
            for hh in (h0, h0 + 1):
                seg = acum[:, hh:hh + 1] - acum_t[hh:hh + 1, :]
                decay = jnp.exp(jnp.where(causal, seg, -jnp.inf))
                ws.append((cb * decay * dt_t[hh:hh + 1, :]).astype(BF16))
            pcols = slice(h0 * HEAD_DIM, (h0 + 2) * HEAD_DIM)
            xp = act_ref[rows, pcols]
            rhs = jnp.concatenate([jnp.where(first_of_pair, xp, 0.0), jnp.where(first_of_pair, 0.0, xp)], axis=0)
            yd = jnp.dot(jnp.concatenate(ws, axis=1), rhs.astype(BF16), preferred_element_type=F32)
            if y_carried is not None:
                yd = yd + y_carried[:, 2 * pair * HEAD_DIM:2 * (pair + 1) * HEAD_DIM]
            ybuf_ref[rows, pcols] = yd


def _ssd_epilogue(x, gate, act_ref, ybuf_ref, d_cols, norm_w, wout, fnw_ref, final):
    y = ybuf_ref[...] + d_cols * act_ref[:, :D_INNER]
    y = _rms(y * gate, norm_w)
    out = x + jnp.dot(y.astype(BF16), _wcols(wout, 0, D_MODEL), preferred_element_type=F32)
    return _rms(out, fnw_ref[...]) if final else out


def _causal_conv(ext, cw_ref, cb_ref, j, cols=slice(None)):
    out = ext(0) * cw_ref[j, 0:1, cols]
    for k in range(1, CONV_K):
        out = out + ext(k) * cw_ref[j, k:k + 1, cols]
    return _silu(out + cb_ref[j:j + 1, cols])


def _ssd_prompt_phases(x_ref, params, masks, outs, scratch, *, tl, i, j, final):
    nw_ref, wz, wxbc, wdt_ref, cw_ref, cb_ref, dtb_ref, alog_ref, dx_ref, normw_ref, wout, fnw_ref, r3_ref = params
    tril_ref, triu_ref, ones_ref, eye_ref = masks
    xo_ref, cst_ref, ssm_ref = outs
    cext_ref, act_ref, ybuf_ref, ht_ref, gate_ref = scratch

    def first():
        cext_ref[0:SUBLANES, :] = jnp.zeros((SUBLANES, CONV_DIM), F32)
        ht_ref[...] = jnp.zeros_like(ht_ref)

    def last():
        cst_ref[0] = cext_ref[SUBLANES - (CONV_K - 1):SUBLANES, :]
        ssm_ref[0, 0] = ht_ref[...].T

    def main():
        _ssd_prompt_main(x_ref, nw_ref, wz, wxbc, wdt_ref, cw_ref, cb_ref, dtb_ref, alog_ref, dx_ref, normw_ref, wout,
                         fnw_ref, r3_ref, tril_ref, triu_ref, ones_ref, eye_ref, xo_ref, cext_ref, act_ref, ybuf_ref,
                         ht_ref, gate_ref, tl=tl, i=i, j=j, final=final)

    return first, main, last


def _ssd_prompt_main(x_ref, nw_ref, wz, wxbc, wdt_ref, cw_ref, cb_ref, dtb_ref, alog_ref, dx_ref, normw_ref, wout,
                     fnw_ref, r3_ref, tril_ref, triu_ref, ones_ref, eye_ref, xo_ref, cext_ref, act_ref, ybuf_ref,
                     ht_ref, gate_ref, *, tl, i, j, final):
    h = _rms(x_ref[0], nw_ref[i:i + 1, :]).astype(BF16)
    dt, a = _dt_and_decay(jnp.dot(h, wdt_ref[:, 0:N_HEADS], preferred_element_type=F32), dtb_ref, alog_ref, j)
    lo = SUBLANES - (CONV_K - 1)
    for sl in range(N_WXBC):
        cols = slice(sl * SLAB, (sl + 1) * SLAB)
        cext_ref[SUBLANES:SUBLANES + tl, cols] = jnp.dot(h, wxbc[sl][...], preferred_element_type=F32)
        act_ref[:, cols] = _causal_conv(lambda k: cext_ref[lo + k:lo + k + tl, cols], cw_ref, cb_ref, j, cols)

    tril, triu, ones, eye = tril_ref[...], triu_ref[...], ones_ref[...], eye_ref[...]
    n_chunks = tl // CHUNK
    zw = D_INNER // (n_chunks * N_GROUPS)
    for c in range(n_chunks):
        r0 = c * CHUNK
        rows = slice(r0, r0 + CHUNK)
        a_c, dt_c = a[rows], dt[rows]

        def gate_piece(g, c=c):
            z0 = (c * N_GROUPS + g) * zw
            gate_ref[:, z0:z0 + zw] = _silu(jnp.dot(h, _wcols(wz, z0, zw), preferred_element_type=F32))

        sums = _ssd_block_sums(a_c, dt_c, tril, triu, ones, eye)
        acum, _, alast, _ = sums
        scales = jnp.concatenate(
            [jnp.exp(acum), jnp.exp(alast - acum) * dt_c, jnp.exp(alast[0:SUBLANES])], axis=0)
        scales_x = _expand_heads(scales, r3_ref[...])

        def carried(g, bg, cg, rows=rows, scales_x=scales_x):
            gcols = slice(g * GROUP_W, (g + 1) * GROUP_W)
            hprev = ht_ref[:, gcols]
            y_off = jnp.dot(cg, hprev.astype(BF16), preferred_element_type=F32) * scales_x[0:CHUNK, gcols]
            xs = (act_ref[rows, gcols] * scales_x[CHUNK:2 * CHUNK, gcols]).astype(BF16)
            st = lax.dot_general(bg, xs, (((0,), (0,)), ((), ())), preferred_element_type=F32)
            ht_ref[:, gcols] = hprev * scales_x[2 * CHUNK:2 * CHUNK + 1, gcols] + st
            return y_off

        _ssd_block_diag(r0, sums, act_ref, ybuf_ref, CHUNK, gate_piece, carried)

    xo_ref[0] = _ssd_epilogue(x_ref[0], gate_ref[...], act_ref, ybuf_ref, dx_ref[j:j + 1, :], normw_ref[j:j + 1, :],
                              wout, fnw_ref, final)
    cext_ref[0:SUBLANES, :] = cext_ref[tl:tl + SUBLANES, :]


def _ssd_sample_phases(x_ref, cin_ref, hin_ref, params, masks, outs, scratch, *, bt, sb, seq, i, j, final):
    nw_ref, wz, wxbc, wdt_ref, cw_ref, cb_ref, dtb_ref, alog_ref, dx_ref, normw_ref, wout, fnw_ref, r3_ref = params
    tril_ref, triu_ref, ones_ref, eye_ref = masks
    xo_ref, cst_ref, hout_ref = outs
    cext_ref, act_ref, ybuf_ref, eax_ref, xsc_ref, cd_ref, h_ref = scratch
    s_idx = pl.program_id(1)
    rows_total = bt * seq

    def first():
        h = _rms(x_ref[...], nw_ref[i:i + 1, :]).astype(BF16)
        h_ref[...] = h
        xbc = jnp.dot(h, _wcols(wxbc, 0, CONV_DIM), preferred_element_type=F32)
        dt, a = _dt_and_decay(jnp.dot(h, wdt_ref[:, 0:N_HEADS], preferred_element_type=F32), dtb_ref, alog_ref, j)
        lo = SUBLANES - (CONV_K - 1)
        for k in range(CONV_K - 1):
            cext_ref[:, lo + k, :] = cin_ref[0, k]
        cext_ref[:, SUBLANES:SUBLANES + seq, :] = xbc.reshape(bt, seq, CONV_DIM)
        conv = _causal_conv(lambda k: cext_ref[:, lo + k:lo + k + seq, :], cw_ref, cb_ref, j)
        act_ref[...] = conv.reshape(rows_total, CONV_DIM)
        for k in range(CONV_K - 1):
            cst_ref[0, k] = cext_ref[:, SUBLANES + seq - (CONV_K - 1) + k, :]
        tril, triu, ones, eye = tril_ref[...], triu_ref[...], ones_ref[...], eye_ref[...]
        for c in range(rows_total // CHUNK):
            r0 = c * CHUNK
            rows = slice(r0, r0 + CHUNK)
            a_c, dt_c = a[rows], dt[rows]
            sums = _ssd_block_sums(a_c, dt_c, tril, triu, ones, eye)
            acum, _, alast, _ = sums
            _ssd_block_diag(r0, sums, act_ref, ybuf_ref, seq)
            scales = jnp.concatenate([jnp.exp(acum), jnp.exp(alast - acum) * dt_c], axis=0)
            scales_x = _expand_heads(scales, r3_ref[...])
            eax_ref[rows, :] = scales_x[0:CHUNK]
            xsc_ref[rows, :] = act_ref[rows, 0:D_INNER] * scales_x[CHUNK:2 * CHUNK]
            cdt = jnp.exp(_dot01_tn(a_c, ones))
            for b in range(CHUNK // seq):
                cd_ref[c * (CHUNK // seq) + b] = jnp.broadcast_to(cdt[:, b * seq:b * seq + 1], (N_HEADS, D_STATE))

    def main():
        for si in range(sb):
            _ssd_sample_sequence(si, s_idx * sb + si)

    def _ssd_sample_sequence(si, b):
        rows = pl.ds(pl.multiple_of(b * seq, seq), seq)
        cd = cd_ref[b]
        for g in range(N_GROUPS):
            gcols = slice(g * GROUP_W, (g + 1) * GROUP_W)
            b0 = D_INNER + g * D_STATE
            c0 = D_INNER + N_GROUPS * D_STATE + g * D_STATE
            bg = act_ref[rows, b0:b0 + D_STATE].astype(BF16)
            cg = act_ref[rows, c0:c0 + D_STATE].astype(BF16)
            hprev = hin_ref[0, si, g * GROUP_W:(g + 1) * GROUP_W, :]
            y_off = lax.dot_general(cg, hprev.astype(BF16), (((1,), (1,)), ((), ())), preferred_element_type=F32)
            ybuf_ref[rows, gcols] = ybuf_ref[rows, gcols] + y_off * eax_ref[rows, gcols]
            st = lax.dot_general(xsc_ref[rows, gcols].astype(BF16), bg, (((0,), (0,)), ((), ())),
                                 preferred_element_type=F32)
            for e in range(HEADS_PER_GROUP):
                hh = g * HEADS_PER_GROUP + e
                hrows = slice(hh * HEAD_DIM, (hh + 1) * HEAD_DIM)
                hout_ref[0, si, hrows, :] = (hin_ref[0, si, hrows, :] * cd[hh:hh + 1, :]
                                             + st[e * HEAD_DIM:(e + 1) * HEAD_DIM, :])

    def last():
        gate = _silu(jnp.dot(h_ref[...], _wcols(wz, 0, D_INNER), preferred_element_type=F32))
        xo_ref[...] = _ssd_epilogue(x_ref[...], gate, act_ref, ybuf_ref, dx_ref[j:j + 1, :], normw_ref[j:j + 1, :],
                                    wout, fnw_ref, final)

    return first, main, last


def _ssd_kernel(*refs, tl, nt, bt, sb, seq, i, j, final, n_prev):
    (xp_ref, xs_ref, cin_ref, hin_ref, *params, masks_p, masks_s, _, outs_p, outs_s, scratch_p, scratch_s) = _split(
        refs, None, None, None, None, *_SSD_PARAM_COUNTS, N_MASKS, N_MASKS, n_prev, 3, 3, 5, 7)
    prompt = _ssd_prompt_phases(xp_ref, params, masks_p, outs_p, scratch_p, tl=tl, i=i, j=j, final=final)
    sample = _ssd_sample_phases(xs_ref, cin_ref, hin_ref, params, masks_s, outs_s, scratch_s, bt=bt, sb=sb, seq=seq,
                                i=i, j=j, final=final)
    t = pl.program_id(1)
    for when, phase in ((t == 0, 0), (None, 1), (t == nt - 1, 2)):
        def run(phase=phase):
            prompt[phase]()
            sample[phase]()
        run() if when is None else pl.when(when)(run)


def _ssd_layer(xp, xs, state_conv, state_ssm, params, i, j, prev, *, seq, final):
    b, l, _ = xp.shape
    tl = SSD_TL
    nt = l // tl
    rows = xs.shape[0]
    nb = rows // seq
    bt = nb // b
    sb = bt // nt
    assert bt * b == nb and sb * nt == bt and (bt * seq) % CHUNK == 0
    hp = N_HEADS * HEAD_DIM
    p_operands, p_specs = _ssd_param_operands(params, j)
    masks = _ssd_block_masks(CHUNK) + _ssd_block_masks(seq)
    operands = [xp, xs, state_conv, state_ssm] + p_operands + masks
    prev = [] if prev is None else list(prev)
    aliases = {len(operands) + k: out for k, out in zip(range(len(prev)), (2, 4, 5))}
    n_layers = state_ssm.shape[0]
    conv_spec = pl.BlockSpec((1, CONV_K - 1, bt, CONV_DIM), lambda s, t: (j, 0, s, 0))
    ssm_spec = pl.BlockSpec((1, sb, hp, D_STATE), lambda s, t: (j, s * nt + t, 0, 0))
    xo, conv_p, ssm_p, xso, conv_s, ssm_s = pl.pallas_call(
        functools.partial(_ssd_kernel, tl=tl, nt=nt, bt=bt, sb=sb, seq=seq, i=i, j=j, final=final, n_prev=len(prev)),
        grid=(b, nt),
        in_specs=[
            pl.BlockSpec((1, tl, D_MODEL), lambda s, t: (s, t, 0)),
            pl.BlockSpec((bt * seq, D_MODEL), lambda s, t: (s, 0)),
            conv_spec,
            ssm_spec,
        ] + p_specs + [_const_spec(m.shape) for m in masks] + [pl.BlockSpec(memory_space=pl.ANY)] * len(prev),
        out_specs=[
            pl.BlockSpec((1, tl, D_MODEL), lambda s, t: (s, t, 0)),
            pl.BlockSpec((1, CONV_K - 1, CONV_DIM), lambda s, t: (s, 0, 0)),
            pl.BlockSpec((1, 1, hp, D_STATE), lambda s, t: (j, s, 0, 0)),
            pl.BlockSpec((bt * seq, D_MODEL), lambda s, t: (s, 0)),
            conv_spec,
            ssm_spec,
        ],
        out_shape=[
            jax.ShapeDtypeStruct((b, l, D_MODEL), F32),
            jax.ShapeDtypeStruct((b, CONV_K - 1, CONV_DIM), F32),
            jax.ShapeDtypeStruct((n_layers, b, hp, D_STATE), F32),
            jax.ShapeDtypeStruct((rows, D_MODEL), F32),
            jax.ShapeDtypeStruct((n_layers, CONV_K - 1, nb, CONV_DIM), F32),
            jax.ShapeDtypeStruct((n_layers, nb, hp, D_STATE), F32),
        ],
        input_output_aliases=aliases,
        scratch_shapes=[
            pltpu.VMEM((tl + SUBLANES, CONV_DIM), F32),
            pltpu.VMEM((tl, CONV_DIM), F32),
            pltpu.VMEM((tl, D_INNER), F32),
            pltpu.VMEM((D_STATE, hp), F32),
            pltpu.VMEM((tl, D_INNER), F32),
            pltpu.VMEM((bt, SUBLANES + seq, CONV_DIM), F32),
            pltpu.VMEM((bt * seq, CONV_DIM), F32),
            pltpu.VMEM((bt * seq, D_INNER), F32),
            pltpu.VMEM((bt * seq, D_INNER), F32),
            pltpu.VMEM((bt * seq, D_INNER), F32),
            pltpu.VMEM((bt, N_HEADS, D_STATE), F32),
            pltpu.VMEM((bt * seq, D_MODEL), BF16),
        ],
        compiler_params=pltpu.CompilerParams(
            dimension_semantics=("arbitrary", "arbitrary"), vmem_limit_bytes=VMEM_LIMIT_BYTES),
        name="ssd_layer",
    )(*operands, *prev)
    return xo, conv_p, xso, (ssm_p, conv_s, ssm_s)


def kernel(x_prompt, x_sample, state_pool, state_conv, state_ssm, norm_w, pool_in_w, pool_mix_w, pool_scale,
           pool_out_w, ssd_in_w, ssd_conv_w, ssd_conv_b, ssd_dt_bias, ssd_A_log, ssd_D, ssd_norm_w, ssd_out_w,
           final_norm_w):
    nb, seq, _ = x_sample.shape
    hp = N_HEADS * HEAD_DIM
    xp = x_prompt
    xs = x_sample.reshape(nb * seq, D_MODEL)
    ssm_in = state_ssm.reshape(state_ssm.shape[0], nb, hp, D_STATE)
    pool_in = state_pool.transpose(0, 2, 1, 3)
    conv_in = state_conv.transpose(0, 2, 1, 3)
    pool_params = (norm_w, pool_in_w.astype(BF16), pool_mix_w.astype(BF16), pool_scale, pool_out_w.astype(BF16))
    ssd_params = (norm_w, ssd_in_w.astype(BF16), ssd_conv_w, ssd_conv_b, ssd_dt_bias, ssd_A_log,
                  jnp.repeat(ssd_D, HEAD_DIM, axis=1), ssd_norm_w, ssd_out_w.astype(BF16),
                  final_norm_w.reshape(1, D_MODEL))
    pool_p, conv_p = [], []
    pool_s = ssd_s = None
    for i in range(DEPTH):
        j = i // 2
        if i % 2 == 0:
            xp, st = _pool_prompt(xp, pool_params, i, j)
            pool_p.append(st)
            xs, pool_s = _pool_sample(xs, pool_in, pool_params, i, j, pool_s, seq=seq)
        else:
            final = i == DEPTH - 1
            xp, cst, xs, ssd_s = _ssd_layer(xp, xs, conv_in, ssm_in, ssd_params, i, j, ssd_s, seq=seq, final=final)
            conv_p.append(cst)
    ssm_p, conv_s, ssm_s = ssd_s
    return (xp, xs.reshape(nb, seq, D_MODEL), jnp.stack(pool_p), pool_s.transpose(0, 2, 1, 3), jnp.stack(conv_p),
            conv_s.transpose(0, 2, 1, 3),
            ssm_p.reshape((-1,) + x_prompt.shape[:1] + state_ssm.shape[2:]), ssm_s.reshape(state_ssm.shape))
```

```python
import functools

import jax
import jax.numpy as jnp
from jax import lax
from jax.experimental import pallas as pl
from jax.experimental.pallas import tpu as pltpu

D_MODEL = 1024
DEPTH = 4
PAST_LEN = 16384
D_INNER = 2 * D_MODEL
POOL_WINDOWS = (2, 4, 8, 16)
POOL_GROUP = D_INNER // len(POOL_WINDOWS)
POOL_BUF = max(POOL_WINDOWS) - 1
HEAD_DIM = 64
N_HEADS = D_INNER // HEAD_DIM
D_STATE = 128
N_GROUPS = 4
HEADS_PER_GROUP = N_HEADS // N_GROUPS
GROUP_W = HEADS_PER_GROUP * HEAD_DIM
CONV_K = 4
CONV_DIM = D_INNER + 2 * N_GROUPS * D_STATE
CHUNK = 128
EPS = 1e-6
LOG2_E = 1.4426950408889634

F32 = jnp.float32
BF16 = jnp.bfloat16

SUBLANES = 8
LANES = 128
SLAB = 512
VMEM_LIMIT_BYTES = 60 * 1024 * 1024

N_WIN = 2 * D_INNER // SLAB
N_WOUT = D_MODEL // SLAB
N_WZ = D_INNER // SLAB
N_WXBC = CONV_DIM // SLAB
DT_LANE_BLOCK = (D_INNER + CONV_DIM) // LANES

POOL_TL = 512
SSD_TL = 256
POOL_BT = 32


def _const_spec(shape):
    zeros = (0,) * len(shape)
    return pl.BlockSpec(shape, lambda *_: zeros, pipeline_mode=pl.Buffered(1))


def _layer_spec(shape, layer):
    idx = (layer,) + (0,) * len(shape)
    return pl.BlockSpec((None,) + tuple(shape), lambda *_: idx, pipeline_mode=pl.Buffered(1))


def _slab_specs(rows, layer, first, count):
    def spec(c):
        return pl.BlockSpec((None, rows, SLAB), lambda *_: (layer, 0, c), pipeline_mode=pl.Buffered(1))
    return [spec(first + c) for c in range(count)]


def _split(refs, *counts):
    out, pos = [], 0
    for n in counts:
        if n is None:
            out.append(refs[pos])
            pos += 1
        else:
            out.append(refs[pos:pos + n])
            pos += n
    assert pos == len(refs)
    return out


def _wcols(slabs, c0, width, krows=slice(None)):
    s0, off = divmod(c0, SLAB)
    if width <= SLAB:
        return slabs[s0][krows, off:off + width]
    return jnp.concatenate([slabs[s0 + i][krows, :] for i in range(width // SLAB)], axis=1)


def _rms(x, w):
    return x * lax.rsqrt(jnp.mean(x * x, axis=-1, keepdims=True) + EPS) * w


def _silu(x):
    half = 0.5 * x
    return half + half * jnp.tanh(half)


def _softplus(x):
    return jnp.maximum(x, 0.0) + jnp.log1p(jnp.exp(-jnp.abs(x)))


def _split3(v):
    v1 = v.astype(BF16).astype(F32)
    r1 = v - v1
    v2 = r1.astype(BF16).astype(F32)
    v3 = (r1 - v2).astype(BF16).astype(F32)
    return v1, v2, v3


def _sum3(f, v):
    p1, p2, p3 = (f(p.astype(BF16)) for p in _split3(v))
    return p1 + p2 + p3


def _dot01(m01, v):
    return _sum3(lambda p: jnp.dot(m01, p, preferred_element_type=F32), v)


def _dot01_tn(v, m01):
    dn = (((0,), (0,)), ((), ()))
    return _sum3(lambda p: lax.dot_general(p, m01, dn, preferred_element_type=F32), v)


def _expand_heads(v, r3):
    parts = jnp.concatenate(_split3(v), axis=1).astype(BF16)
    return jnp.dot(parts, r3, preferred_element_type=F32)


_POOL_WEIGHT_COUNTS = (None, N_WIN, None, None, N_WOUT)


def _pool_weight_operands(params, i, j):
    norm_w, win, wmix, scale, wout = params
    operands = [norm_w] + [win] * N_WIN + [wmix, scale] + [wout] * N_WOUT
    specs = ([_const_spec(norm_w.shape)] + _slab_specs(D_MODEL, j, 0, N_WIN)
             + [_layer_spec(wmix.shape[1:], j), _const_spec(scale.shape)] + _slab_specs(D_INNER, j, 0, N_WOUT))
    return operands, specs


def _reorder_rows(perm, v):
    return jnp.dot(perm, v, preferred_element_type=F32).astype(BF16)


def _pool_tail(p_of_group, z_of_group, wmix_ref, scale, wout, unperm=None):
    acc = None
    for g in range(len(POOL_WINDOWS)):
        cols = slice(g * POOL_GROUP, (g + 1) * POOL_GROUP)
        mixed = jnp.dot(p_of_group(g).astype(BF16), wmix_ref[g], preferred_element_type=F32)
        y = (mixed * scale[:, cols] * _silu(z_of_group(g))).astype(BF16)
        if unperm is not None:
            y = _reorder_rows(unperm, y)
        part = jnp.dot(y, _wcols(wout, 0, D_MODEL, cols), preferred_element_type=F32)
        acc = part if acc is None else acc + part
    return acc


def _pool_prompt_kernel(*refs, tl, nt, i, j):
    x_ref, nw_ref, win, wmix_ref, scale_ref, wout, xo_ref, st_ref, ext_ref = _split(
        refs, None, *_POOL_WEIGHT_COUNTS, None, None, None)
    t = pl.program_id(1)
    hist = 2 * SUBLANES
    x = x_ref[0]

    @pl.when(t == 0)
    def _():
        ext_ref[0:hist, :] = jnp.zeros((hist, D_INNER), F32)

    half = tl // 2
    hs = []
    for r0 in (0, half):
        hs.append(_rms(x_ref[0, r0:r0 + half, :], nw_ref[i:i + 1, :]).astype(BF16))
        ext_ref[hist + r0:hist + r0 + half, :] = jnp.dot(hs[-1], _wcols(win, 0, D_INNER), preferred_element_type=F32)
    h = jnp.concatenate(hs, axis=0)
    pos = t * tl + lax.broadcasted_iota(jnp.int32, (tl, 1), 0)

    def p_of_group(g):
        w = POOL_WINDOWS[g]
        cols = slice(g * POOL_GROUP, (g + 1) * POOL_GROUP)
        u = ext_ref[hist:hist + tl, cols]
        s = u
        for k in range(1, w):
            s = s + ext_ref[hist - k:hist - k + tl, cols]
        cnt = jnp.minimum(pos + 1, w).astype(F32)
        return s * (1.0 / cnt) - u

    def z_of_group(g):
        return jnp.dot(h, _wcols(win, D_INNER + g * POOL_GROUP, POOL_GROUP), preferred_element_type=F32)

    xo_ref[0] = x + _pool_tail(p_of_group, z_of_group, wmix_ref, scale_ref[j:j + 1, :], wout)

    @pl.when(t == nt - 1)
    def _():
        st_ref[0] = ext_ref[hist + tl - POOL_BUF:hist + tl, :]

    ext_ref[0:hist, :] = ext_ref[tl:tl + hist, :]


def _pool_prompt(x, params, i, j):
    b, l, _ = x.shape
    tl = POOL_TL
    nt = l // tl
    w_operands, w_specs = _pool_weight_operands(params, i, j)
    return pl.pallas_call(
        functools.partial(_pool_prompt_kernel, tl=tl, nt=nt, i=i, j=j),
        grid=(b, nt),
        in_specs=[pl.BlockSpec((1, tl, D_MODEL), lambda s, t: (s, t, 0))] + w_specs,
        out_specs=[
            pl.BlockSpec((1, tl, D_MODEL), lambda s, t: (s, t, 0)),
            pl.BlockSpec((1, POOL_BUF, D_INNER), lambda s, t: (s, 0, 0)),
        ],
        out_shape=[
            jax.ShapeDtypeStruct((b, l, D_MODEL), F32),
            jax.ShapeDtypeStruct((b, POOL_BUF, D_INNER), F32),
        ],
        scratch_shapes=[pltpu.VMEM((tl + 2 * SUBLANES, D_INNER), F32)],
        compiler_params=pltpu.CompilerParams(
            dimension_semantics=("arbitrary", "arbitrary"), vmem_limit_bytes=VMEM_LIMIT_BYTES),
        name="pool_prompt",
    )(x, *w_operands)


def _pool_sample_kernel(*refs, bt, seq, start, i, j, n_prev):
    x_ref, buf_ref, perm_ref, unperm_ref, nw_ref, win, wmix_ref, scale_ref, wout, _, xo_ref, st_ref, u_ref = _split(
        refs, None, None, None, None, *_POOL_WEIGHT_COUNTS, n_prev, None, None, None)
    x = x_ref[...]
    h = _reorder_rows(perm_ref[...], _rms(x, nw_ref[i:i + 1, :]).astype(BF16))
    u_ref[...] = jnp.dot(h, _wcols(win, 0, D_INNER), preferred_element_type=F32).reshape(seq, bt, D_INNER)

    def src(t, cols):
        return u_ref[t, :, cols] if t >= 0 else buf_ref[0, POOL_BUF + t, :, cols]

    def p_of_group(g):
        w = POOL_WINDOWS[g]
        cols = slice(g * POOL_GROUP, (g + 1) * POOL_GROUP)
        outs = []
        for t in range(seq):
            cur = src(t, cols)
            s = cur
            for k in range(1, w):
                s = s + src(t - k, cols)
            outs.append(s * (1.0 / min(start + t + 1, w)) - cur)
        return jnp.concatenate(outs, axis=0)

    def z_of_group(g):
        return jnp.dot(h, _wcols(win, D_INNER + g * POOL_GROUP, POOL_GROUP), preferred_element_type=F32)

    xo_ref[...] = x + _pool_tail(p_of_group, z_of_group, wmix_ref, scale_ref[j:j + 1, :], wout, unperm_ref[...])
    keep = POOL_BUF - seq
    st_ref[0, 0:keep] = buf_ref[0, seq:POOL_BUF]
    st_ref[0, keep:POOL_BUF] = u_ref[...]


def _token_major_perm(bt, seq):
    r = jnp.arange(bt * seq)
    perm = (r[None, :] == ((r % bt) * seq + r // bt)[:, None]).astype(BF16)
    return perm, perm.T


def _pool_sample(x, state, params, i, j, prev, *, seq):
    rows = x.shape[0]
    nb = rows // seq
    bt = POOL_BT
    w_operands, w_specs = _pool_weight_operands(params, i, j)
    perms = _token_major_perm(bt, seq)
    operands = [x, state, *perms] + w_operands
    prev = [] if prev is None else [prev]
    aliases = {len(operands) + k: 1 + k for k in range(len(prev))}
    state_spec = pl.BlockSpec((1, POOL_BUF, bt, D_INNER), lambda s: (j, 0, s, 0))
    return pl.pallas_call(
        functools.partial(_pool_sample_kernel, bt=bt, seq=seq, start=PAST_LEN, i=i, j=j, n_prev=len(prev)),
        grid=(nb // bt,),
        in_specs=[pl.BlockSpec((bt * seq, D_MODEL), lambda s: (s, 0)), state_spec]
        + [_const_spec(p.shape) for p in perms] + w_specs + [pl.BlockSpec(memory_space=pl.ANY)] * len(prev),
        out_specs=[pl.BlockSpec((bt * seq, D_MODEL), lambda s: (s, 0)), state_spec],
        out_shape=[jax.ShapeDtypeStruct((rows, D_MODEL), F32), jax.ShapeDtypeStruct(state.shape, F32)],
        input_output_aliases=aliases,
        scratch_shapes=[pltpu.VMEM((seq, bt, D_INNER), F32)],
        compiler_params=pltpu.CompilerParams(
            dimension_semantics=("arbitrary",), vmem_limit_bytes=VMEM_LIMIT_BYTES),
        name="pool_sample",
    )(*operands, *prev)


_SSD_PARAM_COUNTS = (None, N_WZ, N_WXBC) + (None,) * 7 + (N_WOUT,) + (None,) * 2
N_MASKS = 4


def _ssd_block_masks(q):
    i = jnp.arange(CHUNK)[:, None]
    j = jnp.arange(CHUNK)[None, :]
    same = (i // q) == (j // q)
    tril = (same & (j <= i)).astype(BF16)
    ones = same.astype(BF16)
    eye = (i == j).astype(BF16)
    return [tril, tril.T, ones, eye]


def _head_expander():
    h = jnp.arange(N_HEADS)[:, None]
    c = jnp.arange(D_INNER)[None, :] // HEAD_DIM
    r = (h == c).astype(BF16)
    return jnp.concatenate([r, r, r], axis=0)


def _ssd_param_operands(params, j):
    norm_w, w_in, conv_w, conv_b, dt_bias, a_log, d_cols, ssd_norm_w, w_out, final_norm_w = params
    small = [conv_w, conv_b, dt_bias, a_log, d_cols, ssd_norm_w]
    tail = [final_norm_w, _head_expander()]
    operands = [norm_w] + [w_in] * (N_WZ + N_WXBC + 1) + small + [w_out] * N_WOUT + tail
    dt_spec = pl.BlockSpec((None, D_MODEL, LANES), lambda *_: (j, 0, DT_LANE_BLOCK), pipeline_mode=pl.Buffered(1))
    specs = ([_const_spec(norm_w.shape)] + _slab_specs(D_MODEL, j, 0, N_WZ) + _slab_specs(D_MODEL, j, N_WZ, N_WXBC)
             + [dt_spec] + [_const_spec(a.shape) for a in small] + _slab_specs(D_INNER, j, 0, N_WOUT)
             + [_const_spec(a.shape) for a in tail])
    return operands, specs


def _dt_and_decay(dt_raw, dtb_ref, alog_ref, j):
    dt = _softplus(dt_raw + dtb_ref[j:j + 1, :])
    return dt, dt * (-jnp.exp(alog_ref[j:j + 1, :]))


def _ssd_block_sums(a_c, dt_c, tril, triu, ones, eye):
    acum = _dot01(tril, a_c)
    acum_t = _dot01_tn(a_c, triu)
    alast = _dot01(ones, a_c)
    dt_t = _dot01_tn(dt_c, eye)
    return acum, acum_t, alast, dt_t


def _ssd_block_diag(r0, sums, act_ref, ybuf_ref, q, side_work=None, carried=None):
    acum, acum_t, _, dt_t = sums
    acum2, acum2_t = acum * LOG2_E, acum_t * LOG2_E
    ii = lax.broadcasted_iota(jnp.int32, (CHUNK, CHUNK), 0)
    jj = lax.broadcasted_iota(jnp.int32, (CHUNK, CHUNK), 1)
    causal = jj <= ii
    if q != CHUNK:
        causal = causal & ((ii // q) == (jj // q))
    rows = slice(r0, r0 + CHUNK)
    first_of_pair = lax.broadcasted_iota(jnp.int32, (CHUNK, 2 * HEAD_DIM), 1) < HEAD_DIM
    for g in range(N_GROUPS):
        if side_work is not None:
            side_work(g)
        b0 = D_INNER + g * D_STATE
        c0 = D_INNER + N_GROUPS * D_STATE + g * D_STATE
        bg = act_ref[rows, b0:b0 + D_STATE].astype(BF16)
        cg = act_ref[rows, c0:c0 + D_STATE].astype(BF16)
        cb = lax.dot_general(cg, bg, (((1,), (1,)), ((), ())), preferred_element_type=F32)
        y_carried = None if carried is None else carried(g, bg, cg)
        for pair in range(HEADS_PER_GROUP // 2):
            h0 = g * HEADS_PER_GROUP + 2 * pair
            ws = []
            for hh in (h0, h0 + 1):
                seg = acum2[:, hh:hh + 1] - acum2_t[hh:hh + 1, :]
                decay = jnp.exp2(jnp.where(causal, seg, -jnp.inf))
                ws.append((cb * decay * dt_t[hh:hh + 1, :]).astype(BF16))
            pcols = slice(h0 * HEAD_DIM, (h0 + 2) * HEAD_DIM)
            xp = act_ref[rows, pcols]
            rhs = jnp.concatenate([jnp.where(first_of_pair, xp, 0.0), jnp.where(first_of_pair, 0.0, xp)], axis=0)
            yd = jnp.dot(jnp.concatenate(ws, axis=1), rhs.astype(BF16), preferred_element_type=F32)
            if y_carried is not None:
                yd = yd + y_carried[:, 2 * pair * HEAD_DIM:2 * (pair + 1) * HEAD_DIM]
            ybuf_ref[rows, pcols] = yd


def _ssd_epilogue(x, gate, act_ref, ybuf_ref, d_cols, norm_w, wout, fnw_ref, final):
    y = ybuf_ref[...] + d_cols * act_ref[:, :D_INNER]
    y = _rms(y * gate, norm_w)
    out = x + jnp.dot(y.astype(BF16), _wcols(wout, 0, D_MODEL), preferred_element_type=F32)
    return _rms(out, fnw_ref[...]) if final else out


def _causal_conv(ext, cw_ref, cb_ref, j, cols=slice(None)):
    out = ext(0) * cw_ref[j, 0:1, cols]
    for k in range(1, CONV_K):
        out = out + ext(k) * cw_ref[j, k:k + 1, cols]
    return _silu(out + cb_ref[j:j + 1, cols])


def _ssd_prompt_phases(x_ref, params, masks, outs, scratch, *, tl, i, j, final):
    nw_ref, wz, wxbc, wdt_ref, cw_ref, cb_ref, dtb_ref, alog_ref, dx_ref, normw_ref, wout, fnw_ref, r3_ref = params
    tril_ref, triu_ref, ones_ref, eye_ref = masks
    xo_ref, cst_ref, ssm_ref = outs
    cext_ref, act_ref, ybuf_ref, ht_ref, gate_ref = scratch

    def first():
        cext_ref[0:SUBLANES, :] = jnp.zeros((SUBLANES, CONV_DIM), F32)
        ht_ref[...] = jnp.zeros_like(ht_ref)

    def last():
        cst_ref[0] = cext_ref[SUBLANES - (CONV_K - 1):SUBLANES, :]
        ssm_ref[0, 0] = ht_ref[...].T

    def main():
        _ssd_prompt_main(x_ref, nw_ref, wz, wxbc, wdt_ref, cw_ref, cb_ref, dtb_ref, alog_ref, dx_ref, normw_ref, wout,
                         fnw_ref, r3_ref, tril_ref, triu_ref, ones_ref, eye_ref, xo_ref, cext_ref, act_ref, ybuf_ref,
                         ht_ref, gate_ref, tl=tl, i=i, j=j, final=final)

    return first, main, last


def _ssd_prompt_main(x_ref, nw_ref, wz, wxbc, wdt_ref, cw_ref, cb_ref, dtb_ref, alog_ref, dx_ref, normw_ref, wout,
                     fnw_ref, r3_ref, tril_ref, triu_ref, ones_ref, eye_ref, xo_ref, cext_ref, act_ref, ybuf_ref,
                     ht_ref, gate_ref, *, tl, i, j, final):
    h = _rms(x_ref[0], nw_ref[i:i + 1, :]).astype(BF16)
    dt, a = _dt_and_decay(jnp.dot(h, wdt_ref[:, 0:N_HEADS], preferred_element_type=F32), dtb_ref, alog_ref, j)
    lo = SUBLANES - (CONV_K - 1)
    for sl in range(N_WXBC):
        cols = slice(sl * SLAB, (sl + 1) * SLAB)
        cext_ref[SUBLANES:SUBLANES + tl, cols] = jnp.dot(h, wxbc[sl][...], preferred_element_type=F32)
        act_ref[:, cols] = _causal_conv(lambda k: cext_ref[lo + k:lo + k + tl, cols], cw_ref, cb_ref, j, cols)

    tril, triu, ones, eye = tril_ref[...], triu_ref[...], ones_ref[...], eye_ref[...]
    n_chunks = tl // CHUNK
    zw = D_INNER // (n_chunks * N_GROUPS)
    for c in range(n_chunks):
        r0 = c * CHUNK
        rows = slice(r0, r0 + CHUNK)
        a_c, dt_c = a[rows], dt[rows]

        def gate_piece(g, c=c):
            z0 = (c * N_GROUPS + g) * zw
            gate_ref[:, z0:z0 + zw] = _silu(jnp.dot(h, _wcols(wz, z0, zw), preferred_element_type=F32))

        sums = _ssd_block_sums(a_c, dt_c, tril, triu, ones, eye)
        acum, _, alast, _ = sums
        scales = jnp.concatenate(
            [jnp.exp(acum), jnp.exp(alast - acum) * dt_c, jnp.exp(alast[0:SUBLANES])], axis=0)
        scales_x = _expand_heads(scales, r3_ref[...])

        def carried(g, bg, cg, rows=rows, scales_x=scales_x):
            gcols = slice(g * GROUP_W, (g + 1) * GROUP_W)
            hprev = ht_ref[:, gcols]
            y_off = jnp.dot(cg, hprev.astype(BF16), preferred_element_type=F32) * scales_x[0:CHUNK, gcols]
            xs = (act_ref[rows, gcols] * scales_x[CHUNK:2 * CHUNK, gcols]).astype(BF16)
            st = lax.dot_general(bg, xs, (((0,), (0,)), ((), ())), preferred_element_type=F32)
            ht_ref[:, gcols] = hprev * scales_x[2 * CHUNK:2 * CHUNK + 1, gcols] + st
            return y_off

        _ssd_block_diag(r0, sums, act_ref, ybuf_ref, CHUNK, gate_piece, carried)

    xo_ref[0] = _ssd_epilogue(x_ref[0], gate_ref[...], act_ref, ybuf_ref, dx_ref[j:j + 1, :], normw_ref[j:j + 1, :],
                              wout, fnw_ref, final)
    cext_ref[0:SUBLANES, :] = cext_ref[tl:tl + SUBLANES, :]


def _ssd_sample_phases(x_ref, cin_ref, hin_ref, params, masks, outs, scratch, *, bt, sb, seq, i, j, final):
    nw_ref, wz, wxbc, wdt_ref, cw_ref, cb_ref, dtb_ref, alog_ref, dx_ref, normw_ref, wout, fnw_ref, r3_ref = params
    tril_ref, triu_ref, ones_ref, eye_ref = masks
    xo_ref, cst_ref, hout_ref = outs
    cext_ref, act_ref, ybuf_ref, eax_ref, xsc_ref, cd_ref, h_ref = scratch
    s_idx = pl.program_id(1)
    rows_total = bt * seq

    def first():
        h = _rms(x_ref[...], nw_ref[i:i + 1, :]).astype(BF16)
        h_ref[...] = h
        xbc = jnp.dot(h, _wcols(wxbc, 0, CONV_DIM), preferred_element_type=F32)
        dt, a = _dt_and_decay(jnp.dot(h, wdt_ref[:, 0:N_HEADS], preferred_element_type=F32), dtb_ref, alog_ref, j)
        lo = SUBLANES - (CONV_K - 1)
        for k in range(CONV_K - 1):
            cext_ref[:, lo + k, :] = cin_ref[0, k]
        cext_ref[:, SUBLANES:SUBLANES + seq, :] = xbc.reshape(bt, seq, CONV_DIM)
        conv = _causal_conv(lambda k: cext_ref[:, lo + k:lo + k + seq, :], cw_ref, cb_ref, j)
        act_ref[...] = conv.reshape(rows_total, CONV_DIM)
        for k in range(CONV_K - 1):
            cst_ref[0, k] = cext_ref[:, SUBLANES + seq - (CONV_K - 1) + k, :]
        tril, triu, ones, eye = tril_ref[...], triu_ref[...], ones_ref[...], eye_ref[...]
        for c in range(rows_total // CHUNK):
            r0 = c * CHUNK
            rows = slice(r0, r0 + CHUNK)
            a_c, dt_c = a[rows], dt[rows]
            sums = _ssd_block_sums(a_c, dt_c, tril, triu, ones, eye)
            acum, _, alast, _ = sums
            _ssd_block_diag(r0, sums, act_ref, ybuf_ref, seq)
            scales = jnp.concatenate([jnp.exp(acum), jnp.exp(alast - acum) * dt_c], axis=0)
            scales_x = _expand_heads(scales, r3_ref[...])
            eax_ref[rows, :] = scales_x[0:CHUNK]
            xsc_ref[rows, :] = act_ref[rows, 0:D_INNER] * scales_x[CHUNK:2 * CHUNK]
            cdt = jnp.exp(_dot01_tn(a_c, ones))
            for b in range(CHUNK // seq):
                cd_ref[c * (CHUNK // seq) + b] = jnp.broadcast_to(cdt[:, b * seq:b * seq + 1], (N_HEADS, D_STATE))

    def main():
        for si in range(sb):
            _ssd_sample_sequence(si, s_idx * sb + si)

    def _ssd_sample_sequence(si, b):
        rows = pl.ds(pl.multiple_of(b * seq, seq), seq)
        cd = cd_ref[b]
        for g in range(N_GROUPS):
            gcols = slice(g * GROUP_W, (g + 1) * GROUP_W)
            b0 = D_INNER + g * D_STATE
            c0 = D_INNER + N_GROUPS * D_STATE + g * D_STATE
            bg = act_ref[rows, b0:b0 + D_STATE].astype(BF16)
            cg = act_ref[rows, c0:c0 + D_STATE].astype(BF16)
            hprev = hin_ref[0, si, g * GROUP_W:(g + 1) * GROUP_W, :]
            y_off = lax.dot_general(cg, hprev.astype(BF16), (((1,), (1,)), ((), ())), preferred_element_type=F32)
            ybuf_ref[rows, gcols] = ybuf_ref[rows, gcols] + y_off * eax_ref[rows, gcols]
            st = lax.dot_general(xsc_ref[rows, gcols].astype(BF16), bg, (((0,), (0,)), ((), ())),
                                 preferred_element_type=F32)
            for e in range(HEADS_PER_GROUP):
                hh = g * HEADS_PER_GROUP + e
                hrows = slice(hh * HEAD_DIM, (hh + 1) * HEAD_DIM)
                hout_ref[0, si, hrows, :] = (hin_ref[0, si, hrows, :] * cd[hh:hh + 1, :]
                                             + st[e * HEAD_DIM:(e + 1) * HEAD_DIM, :])

    def last():
        gate = _silu(jnp.dot(h_ref[...], _wcols(wz, 0, D_INNER), preferred_element_type=F32))
        xo_ref[...] = _ssd_epilogue(x_ref[...], gate, act_ref, ybuf_ref, dx_ref[j:j + 1, :], normw_ref[j:j + 1, :],
                                    wout, fnw_ref, final)

    return first, main, last


def _ssd_kernel(*refs, tl, nt, bt, sb, seq, i, j, final, n_prev):
    (xp_ref, xs_ref, cin_ref, hin_ref, *params, masks_p, masks_s, _, outs_p, outs_s, scratch_p, scratch_s) = _split(
        refs, None, None, None, None, *_SSD_PARAM_COUNTS, N_MASKS, N_MASKS, n_prev, 3, 3, 5, 7)
    prompt = _ssd_prompt_phases(xp_ref, params, masks_p, outs_p, scratch_p, tl=tl, i=i, j=j, final=final)
    sample = _ssd_sample_phases(xs_ref, cin_ref, hin_ref, params, masks_s, outs_s, scratch_s, bt=bt, sb=sb, seq=seq,
                                i=i, j=j, final=final)
    t = pl.program_id(1)
    for when, phase in ((t == 0, 0), (None, 1), (t == nt - 1, 2)):
        def run(phase=phase):
            prompt[phase]()
            sample[phase]()
        run() if when is None else pl.when(when)(run)


def _ssd_layer(xp, xs, state_conv, state_ssm, params, i, j, prev, *, seq, final):
    b, l, _ = xp.shape
    tl = SSD_TL
    nt = l // tl
    rows = xs.shape[0]
    nb = rows // seq
    bt = nb // b
    sb = bt // nt
    assert bt * b == nb and sb * nt == bt and (bt * seq) % CHUNK == 0
    hp = N_HEADS * HEAD_DIM
    p_operands, p_specs = _ssd_param_operands(params, j)
    masks = _ssd_block_masks(CHUNK) + _ssd_block_masks(seq)
    operands = [xp, xs, state_conv, state_ssm] + p_operands + masks
    prev = [] if prev is None else list(prev)
    aliases = {len(operands) + k: out for k, out in zip(range(len(prev)), (2, 4, 5))}
    n_layers = state_ssm.shape[0]
    conv_spec = pl.BlockSpec((1, CONV_K - 1, bt, CONV_DIM), lambda s, t: (j, 0, s, 0))
    ssm_spec = pl.BlockSpec((1, sb, hp, D_STATE), lambda s, t: (j, s * nt + t, 0, 0))
    xo, conv_p, ssm_p, xso, conv_s, ssm_s = pl.pallas_call(
        functools.partial(_ssd_kernel, tl=tl, nt=nt, bt=bt, sb=sb, seq=seq, i=i, j=j, final=final, n_prev=len(prev)),
        grid=(b, nt),
        in_specs=[
            pl.BlockSpec((1, tl, D_MODEL), lambda s, t: (s, t, 0)),
            pl.BlockSpec((bt * seq, D_MODEL), lambda s, t: (s, 0)),
            conv_spec,
            ssm_spec,
        ] + p_specs + [_const_spec(m.shape) for m in masks] + [pl.BlockSpec(memory_space=pl.ANY)] * len(prev),
        out_specs=[
            pl.BlockSpec((1, tl, D_MODEL), lambda s, t: (s, t, 0)),
            pl.BlockSpec((1, CONV_K - 1, CONV_DIM), lambda s, t: (s, 0, 0)),
            pl.BlockSpec((1, 1, hp, D_STATE), lambda s, t: (j, s, 0, 0)),
            pl.BlockSpec((bt * seq, D_MODEL), lambda s, t: (s, 0)),
            conv_spec,
            ssm_spec,
        ],
        out_shape=[
            jax.ShapeDtypeStruct((b, l, D_MODEL), F32),
            jax.ShapeDtypeStruct((b, CONV_K - 1, CONV_DIM), F32),
            jax.ShapeDtypeStruct((n_layers, b, hp, D_STATE), F32),
            jax.ShapeDtypeStruct((rows, D_MODEL), F32),
            jax.ShapeDtypeStruct((n_layers, CONV_K - 1, nb, CONV_DIM), F32),
            jax.ShapeDtypeStruct((n_layers, nb, hp, D_STATE), F32),
        ],
        input_output_aliases=aliases,
        scratch_shapes=[
            pltpu.VMEM((tl + SUBLANES, CONV_DIM), F32),
            pltpu.VMEM((tl, CONV_DIM), F32),
            pltpu.VMEM((tl, D_INNER), F32),
            pltpu.VMEM((D_STATE, hp), F32),
            pltpu.VMEM((tl, D_INNER), F32),
            pltpu.VMEM((bt, SUBLANES + seq, CONV_DIM), F32),
            pltpu.VMEM((bt * seq, CONV_DIM), F32),
            pltpu.VMEM((bt * seq, D_INNER), F32),
            pltpu.VMEM((bt * seq, D_INNER), F32),
            pltpu.VMEM((bt * seq, D_INNER), F32),
            pltpu.VMEM((bt, N_HEADS, D_STATE), F32),
            pltpu.VMEM((bt * seq, D_MODEL), BF16),
        ],
        compiler_params=pltpu.CompilerParams(
            dimension_semantics=("arbitrary", "arbitrary"), vmem_limit_bytes=VMEM_LIMIT_BYTES),
        name="ssd_layer",
    )(*operands, *prev)
    return xo, conv_p, xso, (ssm_p, conv_s, ssm_s)


def kernel(x_prompt, x_sample, state_pool, state_conv, state_ssm, norm_w, pool_in_w, pool_mix_w, pool_scale,
           pool_out_w, ssd_in_w, ssd_conv_w, ssd_conv_b, ssd_dt_bias, ssd_A_log, ssd_D, ssd_norm_w, ssd_out_w,
           final_norm_w):
    nb, seq, _ = x_sample.shape
    hp = N_HEADS * HEAD_DIM
    xp = x_prompt
    xs = x_sample.reshape(nb * seq, D_MODEL)
    ssm_in = state_ssm.reshape(state_ssm.shape[0], nb, hp, D_STATE)
    pool_in = state_pool.transpose(0, 2, 1, 3)
    conv_in = state_conv.transpose(0, 2, 1, 3)
    pool_params = (norm_w, pool_in_w.astype(BF16), pool_mix_w.astype(BF16), pool_scale, pool_out_w.astype(BF16))
    ssd_params = (norm_w, ssd_in_w.astype(BF16), ssd_conv_w, ssd_conv_b, ssd_dt_bias, ssd_A_log,
                  jnp.repeat(ssd_D, HEAD_DIM, axis=1), ssd_norm_w, ssd_out_w.astype(BF16),
                  final_norm_w.reshape(1, D_MODEL))
    pool_p, conv_p = [], []
    pool_s = ssd_s = None
    for i in range(DEPTH):
        j = i // 2
        if i % 2 == 0:
            xp, st = _pool_prompt(xp, pool_params, i, j)
            pool_p.append(st)
            xs, pool_s = _pool_sample(xs, pool_in, pool_params, i, j, pool_s, seq=seq)
        else:
            final = i == DEPTH - 1
            xp, cst, xs, ssd_s = _ssd_layer(xp, xs, conv_in, ssm_in, ssd_params, i, j, ssd_s, seq=seq, final=final)
            conv_p.append(cst)
    ssm_p, conv_s, ssm_s = ssd_s
    return (xp, xs.reshape(nb, seq, D_MODEL), jnp.stack(pool_p), pool_s.transpose(0, 2, 1, 3), jnp.stack(conv_p),
            conv_s.transpose(0, 2, 1, 3),
            ssm_p.reshape((-1,) + x_prompt.shape[:1] + state_ssm.shape[2:]), ssm_s.reshape(state_ssm.shape))
```

```python
import functools

import jax
import jax.numpy as jnp
from jax import lax
from jax.experimental import pallas as pl
from jax.experimental.pallas import tpu as pltpu

D_MODEL = 1024
DEPTH = 4
PAST_LEN = 16384
D_INNER = 2 * D_MODEL
POOL_WINDOWS = (2, 4, 8, 16)
POOL_GROUP = D_INNER // len(POOL_WINDOWS)
POOL_BUF = max(POOL_WINDOWS) - 1
HEAD_DIM = 64
N_HEADS = D_INNER // HEAD_DIM
D_STATE = 128
N_GROUPS = 4
HEADS_PER_GROUP = N_HEADS // N_GROUPS
GROUP_W = HEADS_PER_GROUP * HEAD_DIM
CONV_K = 4
CONV_DIM = D_INNER + 2 * N_GROUPS * D_STATE
CHUNK = 128
EPS = 1e-6
LOG2_E = 1.4426950408889634

F32 = jnp.float32
BF16 = jnp.bfloat16

SUBLANES = 8
LANES = 128
SLAB = 512
VMEM_LIMIT_BYTES = 60 * 1024 * 1024

N_WIN = 2 * D_INNER // SLAB
N_WOUT = D_MODEL // SLAB
N_WZ = D_INNER // SLAB
N_WXBC = CONV_DIM // SLAB
DT_LANE_BLOCK = (D_INNER + CONV_DIM) // LANES

POOL_TL = 512
SSD_TL = 256
POOL_BT = 32


def _const_spec(shape):
    zeros = (0,) * len(shape)
    return pl.BlockSpec(shape, lambda *_: zeros, pipeline_mode=pl.Buffered(1))


def _layer_spec(shape, layer):
    idx = (layer,) + (0,) * len(shape)
    return pl.BlockSpec((None,) + tuple(shape), lambda *_: idx, pipeline_mode=pl.Buffered(1))


def _slab_specs(rows, layer, first, count):
    def spec(c):
        return pl.BlockSpec((None, rows, SLAB), lambda *_: (layer, 0, c), pipeline_mode=pl.Buffered(1))
    return [spec(first + c) for c in range(count)]


def _split(refs, *counts):
    out, pos = [], 0
    for n in counts:
        if n is None:
            out.append(refs[pos])
            pos += 1
        else:
            out.append(refs[pos:pos + n])
            pos += n
    assert pos == len(refs)
    return out


def _wcols(slabs, c0, width, krows=slice(None)):
    s0, off = divmod(c0, SLAB)
    if width <= SLAB:
        return slabs[s0][krows, off:off + width]
    return jnp.concatenate([slabs[s0 + i][krows, :] for i in range(width // SLAB)], axis=1)


def _rms(x, w):
    return x * lax.rsqrt(jnp.mean(x * x, axis=-1, keepdims=True) + EPS) * w


def _silu(x):
    half = 0.5 * x
    return half + half * jnp.tanh(half)


def _softplus(x):
    return jnp.maximum(x, 0.0) + jnp.log1p(jnp.exp(-jnp.abs(x)))


def _split3(v):
    v1 = v.astype(BF16).astype(F32)
    r1 = v - v1
    v2 = r1.astype(BF16).astype(F32)
    v3 = (r1 - v2).astype(BF16).astype(F32)
    return v1, v2, v3


def _sum3(f, v):
    p1, p2, p3 = (f(p.astype(BF16)) for p in _split3(v))
    return p1 + p2 + p3


def _dot01(m01, v):
    return _sum3(lambda p: jnp.dot(m01, p, preferred_element_type=F32), v)


def _dot01_tn(v, m01):
    dn = (((0,), (0,)), ((), ()))
    return _sum3(lambda p: lax.dot_general(p, m01, dn, preferred_element_type=F32), v)


def _expand_heads(v, r3):
    parts = jnp.concatenate(_split3(v), axis=1).astype(BF16)
    return jnp.dot(parts, r3, preferred_element_type=F32)


_POOL_WEIGHT_COUNTS = (None, N_WIN, None, None, N_WOUT)


def _pool_weight_operands(params, i, j):
    norm_w, win, wmix, scale, wout = params
    operands = [norm_w] + [win] * N_WIN + [wmix, scale] + [wout] * N_WOUT
    specs = ([_const_spec(norm_w.shape)] + _slab_specs(D_MODEL, j, 0, N_WIN)
             + [_layer_spec(wmix.shape[1:], j), _const_spec(scale.shape)] + _slab_specs(D_INNER, j, 0, N_WOUT))
    return operands, specs


def _reorder_rows(perm, v):
    return jnp.dot(perm, v, preferred_element_type=F32).astype(BF16)


def _pool_tail(p_of_group, z_of_group, wmix_ref, scale, wout, unperm=None):
    acc = None
    for g in range(len(POOL_WINDOWS)):
        cols = slice(g * POOL_GROUP, (g + 1) * POOL_GROUP)
        mixed = jnp.dot(p_of_group(g).astype(BF16), wmix_ref[g], preferred_element_type=F32)
        y = (mixed * scale[:, cols] * _silu(z_of_group(g))).astype(BF16)
        if unperm is not None:
            y = _reorder_rows(unperm, y)
        part = jnp.dot(y, _wcols(wout, 0, D_MODEL, cols), preferred_element_type=F32)
        acc = part if acc is None else acc + part
    return acc


def _pool_prompt_kernel(*refs, tl, nt, i, j):
    x_ref, nw_ref, win, wmix_ref, scale_ref, wout, xo_ref, st_ref, ext_ref = _split(
        refs, None, *_POOL_WEIGHT_COUNTS, None, None, None)
    t = pl.program_id(1)
    hist = 2 * SUBLANES
    x = x_ref[0]

    @pl.when(t == 0)
    def _():
        ext_ref[0:hist, :] = jnp.zeros((hist, D_INNER), F32)

    half = tl // 2
    hs = []
    for r0 in (0, half):
        hs.append(_rms(x_ref[0, r0:r0 + half, :], nw_ref[i:i + 1, :]).astype(BF16))
        ext_ref[hist + r0:hist + r0 + half, :] = jnp.dot(hs[-1], _wcols(win, 0, D_INNER), preferred_element_type=F32)
    h = jnp.concatenate(hs, axis=0)
    pos = t * tl + lax.broadcasted_iota(jnp.int32, (tl, 1), 0)

    def p_of_group(g):
        w = POOL_WINDOWS[g]
        cols = slice(g * POOL_GROUP, (g + 1) * POOL_GROUP)
        u = ext_ref[hist:hist + tl, cols]
        s = u
        for k in range(1, w):
            s = s + ext_ref[hist - k:hist - k + tl, cols]
        cnt = jnp.minimum(pos + 1, w).astype(F32)
        return s * (1.0 / cnt) - u

    def z_of_group(g):
        return jnp.dot(h, _wcols(win, D_INNER + g * POOL_GROUP, POOL_GROUP), preferred_element_type=F32)

    xo_ref[0] = x + _pool_tail(p_of_group, z_of_group, wmix_ref, scale_ref[j:j + 1, :], wout)

    @pl.when(t == nt - 1)
    def _():
        st_ref[0] = ext_ref[hist + tl - POOL_BUF:hist + tl, :]

    ext_ref[0:hist, :] = ext_ref[tl:tl + hist, :]


def _pool_prompt(x, params, i, j):
    b, l, _ = x.shape
    tl = POOL_TL
    nt = l // tl
    w_operands, w_specs = _pool_weight_operands(params, i, j)
    return pl.pallas_call(
        functools.partial(_pool_prompt_kernel, tl=tl, nt=nt, i=i, j=j),
        grid=(b, nt),
        in_specs=[pl.BlockSpec((1, tl, D_MODEL), lambda s, t: (s, t, 0))] + w_specs,
        out_specs=[
            pl.BlockSpec((1, tl, D_MODEL), lambda s, t: (s, t, 0)),
            pl.BlockSpec((1, POOL_BUF, D_INNER), lambda s, t: (s, 0, 0)),
        ],
        out_shape=[
            jax.ShapeDtypeStruct((b, l, D_MODEL), F32),
            jax.ShapeDtypeStruct((b, POOL_BUF, D_INNER), F32),
        ],
        scratch_shapes=[pltpu.VMEM((tl + 2 * SUBLANES, D_INNER), F32)],
        compiler_params=pltpu.CompilerParams(
            dimension_semantics=("arbitrary", "arbitrary"), vmem_limit_bytes=VMEM_LIMIT_BYTES),
        name="pool_prompt",
    )(x, *w_operands)


def _pool_sample_kernel(*refs, bt, seq, start, i, j, n_prev):
    x_ref, buf_ref, perm_ref, unperm_ref, nw_ref, win, wmix_ref, scale_ref, wout, _, xo_ref, st_ref, u_ref = _split(
        refs, None, None, None, None, *_POOL_WEIGHT_COUNTS, n_prev, None, None, None)
    x = x_ref[...]
    h = _reorder_rows(perm_ref[...], _rms(x, nw_ref[i:i + 1, :]).astype(BF16))
    u_ref[...] = jnp.dot(h, _wcols(win, 0, D_INNER), preferred_element_type=F32).reshape(seq, bt, D_INNER)

    def src(t, cols):
        return u_ref[t, :, cols] if t >= 0 else buf_ref[0, POOL_BUF + t, :, cols]

    def p_of_group(g):
        w = POOL_WINDOWS[g]
        cols = slice(g * POOL_GROUP, (g + 1) * POOL_GROUP)
        outs = []
        for t in range(seq):
            cur = src(t, cols)
            s = cur
            for k in range(1, w):
                s = s + src(t - k, cols)
            outs.append(s * (1.0 / min(start + t + 1, w)) - cur)
        return jnp.concatenate(outs, axis=0)

    def z_of_group(g):
        return jnp.dot(h, _wcols(win, D_INNER + g * POOL_GROUP, POOL_GROUP), preferred_element_type=F32)

    xo_ref[...] = x + _pool_tail(p_of_group, z_of_group, wmix_ref, scale_ref[j:j + 1, :], wout, unperm_ref[...])
    keep = POOL_BUF - seq
    st_ref[0, 0:keep] = buf_ref[0, seq:POOL_BUF]
    st_ref[0, keep:POOL_BUF] = u_ref[...]


def _token_major_perm(bt, seq):
    r = jnp.arange(bt * seq)
    perm = (r[None, :] == ((r % bt) * seq + r // bt)[:, None]).astype(BF16)
    return perm, perm.T


def _pool_sample(x, state, params, i, j, prev, *, seq):
    rows = x.shape[0]
    nb = rows // seq
    bt = POOL_BT
    w_operands, w_specs = _pool_weight_operands(params, i, j)
    perms = _token_major_perm(bt, seq)
    operands = [x, state, *perms] + w_operands
    prev = [] if prev is None else [prev]
    aliases = {len(operands) + k: 1 + k for k in range(len(prev))}
    state_spec = pl.BlockSpec((1, POOL_BUF, bt, D_INNER), lambda s: (j, 0, s, 0))
    return pl.pallas_call(
        functools.partial(_pool_sample_kernel, bt=bt, seq=seq, start=PAST_LEN, i=i, j=j, n_prev=len(prev)),
        grid=(nb // bt,),
        in_specs=[pl.BlockSpec((bt * seq, D_MODEL), lambda s: (s, 0)), state_spec]
        + [_const_spec(p.shape) for p in perms] + w_specs + [pl.BlockSpec(memory_space=pl.ANY)] * len(prev),
        out_specs=[pl.BlockSpec((bt * seq, D_MODEL), lambda s: (s, 0)), state_spec],
        out_shape=[jax.ShapeDtypeStruct((rows, D_MODEL), F32), jax.ShapeDtypeStruct(state.shape, F32)],
        input_output_aliases=aliases,
        scratch_shapes=[pltpu.VMEM((seq, bt, D_INNER), F32)],
        compiler_params=pltpu.CompilerParams(
            dimension_semantics=("arbitrary",), vmem_limit_bytes=VMEM_LIMIT_BYTES),
        name="pool_sample",
    )(*operands, *prev)


_SSD_PARAM_COUNTS = (None, N_WZ, N_WXBC) + (None,) * 7 + (N_WOUT,) + (None,) * 2
N_MASKS = 4


def _ssd_block_masks(q):
    i = jnp.arange(CHUNK)[:, None]
    j = jnp.arange(CHUNK)[None, :]
    same = (i // q) == (j // q)
    tril = (same & (j <= i)).astype(BF16)
    ones = same.astype(BF16)
    eye = (i == j).astype(BF16)
    return [tril, tril.T, ones, eye]


def _head_expander():
    h = jnp.arange(N_HEADS)[:, None]
    c = jnp.arange(D_INNER)[None, :] // HEAD_DIM
    r = (h == c).astype(BF16)
    return jnp.concatenate([r, r, r], axis=0)


def _ssd_param_operands(params, j):
    norm_w, w_in, conv_w, conv_b, dt_bias, a_log, d_cols, ssd_norm_w, w_out, final_norm_w = params
    small = [conv_w, conv_b, dt_bias, a_log, d_cols, ssd_norm_w]
    tail = [final_norm_w, _head_expander()]
    operands = [norm_w] + [w_in] * (N_WZ + N_WXBC + 1) + small + [w_out] * N_WOUT + tail
    dt_spec = pl.BlockSpec((None, D_MODEL, LANES), lambda *_: (j, 0, DT_LANE_BLOCK), pipeline_mode=pl.Buffered(1))
    specs = ([_const_spec(norm_w.shape)] + _slab_specs(D_MODEL, j, 0, N_WZ) + _slab_specs(D_MODEL, j, N_WZ, N_WXBC)
             + [dt_spec] + [_const_spec(a.shape) for a in small] + _slab_specs(D_INNER, j, 0, N_WOUT)
             + [_const_spec(a.shape) for a in tail])
    return operands, specs


def _dt_and_decay(dt_raw, dtb_ref, alog_ref, j):
    dt = _softplus(dt_raw + dtb_ref[j:j + 1, :])
    return dt, dt * (-jnp.exp(alog_ref[j:j + 1, :]))


def _ssd_block_sums(a_c, dt_c, tril, triu, ones, eye):
    acum = _dot01(tril, a_c)
    acum_t = _dot01_tn(a_c, triu)
    alast = _dot01(ones, a_c)
    dt_t = _dot01_tn(dt_c, eye)
    return acum, acum_t, alast, dt_t


def _ssd_block_diag(r0, sums, act_ref, ybuf_ref, q, side_work=None, carried=None):
    acum, acum_t, _, dt_t = sums
    acum2, acum2_t = acum * LOG2_E, acum_t * LOG2_E
    ii = lax.broadcasted_iota(jnp.int32, (CHUNK, CHUNK), 0)
    jj = lax.broadcasted_iota(jnp.int32, (CHUNK, CHUNK), 1)
    causal = jj <= ii
    if q != CHUNK:
        causal = causal & ((ii // q) == (jj // q))
    rows = slice(r0, r0 + CHUNK)
    first_of_pair = lax.broadcasted_iota(jnp.int32, (CHUNK, 2 * HEAD_DIM), 1) < HEAD_DIM
    for g in range(N_GROUPS):
        if side_work is not None:
            side_work(g)
        b0 = D_INNER + g * D_STATE
        c0 = D_INNER + N_GROUPS * D_STATE + g * D_STATE
        bg = act_ref[rows, b0:b0 + D_STATE].astype(BF16)
        cg = act_ref[rows, c0:c0 + D_STATE].astype(BF16)
        cb = lax.dot_general(cg, bg, (((1,), (1,)), ((), ())), preferred_element_type=F32)
        y_carried = None if carried is None else carried(g, bg, cg)
        for pair in range(HEADS_PER_GROUP // 2):
            h0 = g * HEADS_PER_GROUP + 2 * pair
            ws = []
            for hh in (h0, h0 + 1):
                seg = acum2[:, hh:hh + 1] - acum2_t[hh:hh + 1, :]
                decay = jnp.exp2(jnp.where(causal, seg, -jnp.inf))
                ws.append((cb * decay * dt_t[hh:hh + 1, :]).astype(BF16))
            pcols = slice(h0 * HEAD_DIM, (h0 + 2) * HEAD_DIM)
            xp = act_ref[rows, pcols]
            rhs = jnp.concatenate([jnp.where(first_of_pair, xp, 0.0), jnp.where(first_of_pair, 0.0, xp)], axis=0)
            yd = jnp.dot(jnp.concatenate(ws, axis=1), rhs.astype(BF16), preferred_element_type=F32)
            if y_carried is not None:
                yd = yd + y_carried[:, 2 * pair * HEAD_DIM:2 * (pair + 1) * HEAD_DIM]
            ybuf_ref[rows, pcols] = yd


def _ssd_epilogue(x, gate, act_ref, ybuf_ref, d_cols, norm_w, wout, fnw_ref, final):
    y = ybuf_ref[...] + d_cols * act_ref[:, :D_INNER]
    y = _rms(y * gate, norm_w)
    out = x + jnp.dot(y.astype(BF16), _wcols(wout, 0, D_MODEL), preferred_element_type=F32)
    return _rms(out, fnw_ref[...]) if final else out


def _causal_conv(ext, cw_ref, cb_ref, j, cols=slice(None)):
    half = ext(0) * (0.5 * cw_ref[j, 0:1, cols])
    for k in range(1, CONV_K):
        half = half + ext(k) * (0.5 * cw_ref[j, k:k + 1, cols])
    half = half + 0.5 * cb_ref[j:j + 1, cols]
    return half + half * jnp.tanh(half)


def _ssd_prompt_phases(x_ref, params, masks, outs, scratch, *, tl, i, j, final):
    nw_ref, wz, wxbc, wdt_ref, cw_ref, cb_ref, dtb_ref, alog_ref, dx_ref, normw_ref, wout, fnw_ref, r3_ref = params
    tril_ref, triu_ref, ones_ref, eye_ref = masks
    xo_ref, cst_ref, ssm_ref = outs
    cext_ref, act_ref, ybuf_ref, ht_ref, gate_ref = scratch

    def first():
        cext_ref[0:SUBLANES, :] = jnp.zeros((SUBLANES, CONV_DIM), F32)
        ht_ref[...] = jnp.zeros_like(ht_ref)

    def last():
        cst_ref[0] = cext_ref[SUBLANES - (CONV_K - 1):SUBLANES, :]
        ssm_ref[0, 0] = ht_ref[...].T

    def main():
        _ssd_prompt_main(x_ref, nw_ref, wz, wxbc, wdt_ref, cw_ref, cb_ref, dtb_ref, alog_ref, dx_ref, normw_ref, wout,
                         fnw_ref, r3_ref, tril_ref, triu_ref, ones_ref, eye_ref, xo_ref, cext_ref, act_ref, ybuf_ref,
                         ht_ref, gate_ref, tl=tl, i=i, j=j, final=final)

    return first, main, last


def _ssd_prompt_main(x_ref, nw_ref, wz, wxbc, wdt_ref, cw_ref, cb_ref, dtb_ref, alog_ref, dx_ref, normw_ref, wout,
                     fnw_ref, r3_ref, tril_ref, triu_ref, ones_ref, eye_ref, xo_ref, cext_ref, act_ref, ybuf_ref,
                     ht_ref, gate_ref, *, tl, i, j, final):
    h = _rms(x_ref[0], nw_ref[i:i + 1, :]).astype(BF16)
    dt, a = _dt_and_decay(jnp.dot(h, wdt_ref[:, 0:N_HEADS], preferred_element_type=F32), dtb_ref, alog_ref, j)
    lo = SUBLANES - (CONV_K - 1)
    for sl in range(N_WXBC):
        cols = slice(sl * SLAB, (sl + 1) * SLAB)
        cext_ref[SUBLANES:SUBLANES + tl, cols] = jnp.dot(h, wxbc[sl][...], preferred_element_type=F32)
        act_ref[:, cols] = _causal_conv(lambda k: cext_ref[lo + k:lo + k + tl, cols], cw_ref, cb_ref, j, cols)

    tril, triu, ones, eye = tril_ref[...], triu_ref[...], ones_ref[...], eye_ref[...]
    n_chunks = tl // CHUNK
    zw = D_INNER // (n_chunks * N_GROUPS)
    for c in range(n_chunks):
        r0 = c * CHUNK
        rows = slice(r0, r0 + CHUNK)
        a_c, dt_c = a[rows], dt[rows]

        def gate_piece(g, c=c):
            z0 = (c * N_GROUPS + g) * zw
            gate_ref[:, z0:z0 + zw] = _silu(jnp.dot(h, _wcols(wz, z0, zw), preferred_element_type=F32))

        sums = _ssd_block_sums(a_c, dt_c, tril, triu, ones, eye)
        acum, _, alast, _ = sums
        scales = jnp.concatenate(
            [jnp.exp(acum), jnp.exp(alast - acum) * dt_c, jnp.exp(alast[0:SUBLANES])], axis=0)
        scales_x = _expand_heads(scales, r3_ref[...])

        def carried(g, bg, cg, rows=rows, scales_x=scales_x):
            gcols = slice(g * GROUP_W, (g + 1) * GROUP_W)
            hprev = ht_ref[:, gcols]
            y_off = jnp.dot(cg, hprev.astype(BF16), preferred_element_type=F32) * scales_x[0:CHUNK, gcols]
            xs = (act_ref[rows, gcols] * scales_x[CHUNK:2 * CHUNK, gcols]).astype(BF16)
            st = lax.dot_general(bg, xs, (((0,), (0,)), ((), ())), preferred_element_type=F32)
            ht_ref[:, gcols] = hprev * scales_x[2 * CHUNK:2 * CHUNK + 1, gcols] + st
            return y_off

        _ssd_block_diag(r0, sums, act_ref, ybuf_ref, CHUNK, gate_piece, carried)

    xo_ref[0] = _ssd_epilogue(x_ref[0], gate_ref[...], act_ref, ybuf_ref, dx_ref[j:j + 1, :], normw_ref[j:j + 1, :],
                              wout, fnw_ref, final)
    cext_ref[0:SUBLANES, :] = cext_ref[tl:tl + SUBLANES, :]


def _ssd_sample_phases(x_ref, cin_ref, hin_ref, params, masks, outs, scratch, *, bt, sb, seq, i, j, final):
    nw_ref, wz, wxbc, wdt_ref, cw_ref, cb_ref, dtb_ref, alog_ref, dx_ref, normw_ref, wout, fnw_ref, r3_ref = params
    tril_ref, triu_ref, ones_ref, eye_ref = masks
    xo_ref, cst_ref, hout_ref = outs
    cext_ref, act_ref, ybuf_ref, eax_ref, xsc_ref, cd_ref, h_ref = scratch
    s_idx = pl.program_id(1)
    rows_total = bt * seq

    def first():
        h = _rms(x_ref[...], nw_ref[i:i + 1, :]).astype(BF16)
        h_ref[...] = h
        xbc = jnp.dot(h, _wcols(wxbc, 0, CONV_DIM), preferred_element_type=F32)
        dt, a = _dt_and_decay(jnp.dot(h, wdt_ref[:, 0:N_HEADS], preferred_element_type=F32), dtb_ref, alog_ref, j)
        lo = SUBLANES - (CONV_K - 1)
        for k in range(CONV_K - 1):
            cext_ref[:, lo + k, :] = cin_ref[0, k]
        cext_ref[:, SUBLANES:SUBLANES + seq, :] = xbc.reshape(bt, seq, CONV_DIM)
        conv = _causal_conv(lambda k: cext_ref[:, lo + k:lo + k + seq, :], cw_ref, cb_ref, j)
        act_ref[...] = conv.reshape(rows_total, CONV_DIM)
        for k in range(CONV_K - 1):
            cst_ref[0, k] = cext_ref[:, SUBLANES + seq - (CONV_K - 1) + k, :]
        tril, triu, ones, eye = tril_ref[...], triu_ref[...], ones_ref[...], eye_ref[...]
        for c in range(rows_total // CHUNK):
            r0 = c * CHUNK
            rows = slice(r0, r0 + CHUNK)
            a_c, dt_c = a[rows], dt[rows]
            sums = _ssd_block_sums(a_c, dt_c, tril, triu, ones, eye)
            acum, _, alast, _ = sums
            _ssd_block_diag(r0, sums, act_ref, ybuf_ref, seq)
            scales = jnp.concatenate([jnp.exp(acum), jnp.exp(alast - acum) * dt_c], axis=0)
            scales_x = _expand_heads(scales, r3_ref[...])
            eax_ref[rows, :] = scales_x[0:CHUNK]
            xsc_ref[rows, :] = act_ref[rows, 0:D_INNER] * scales_x[CHUNK:2 * CHUNK]
            cdt = jnp.exp(_dot01_tn(a_c, ones))
            for b in range(CHUNK // seq):
                cd_ref[c * (CHUNK // seq) + b] = jnp.broadcast_to(cdt[:, b * seq:b * seq + 1], (N_HEADS, D_STATE))

    def main():
        for si in range(sb):
            _ssd_sample_sequence(si, s_idx * sb + si)

    def _ssd_sample_sequence(si, b):
        rows = pl.ds(pl.multiple_of(b * seq, seq), seq)
        cd = cd_ref[b]
        for g in range(N_GROUPS):
            gcols = slice(g * GROUP_W, (g + 1) * GROUP_W)
            b0 = D_INNER + g * D_STATE
            c0 = D_INNER + N_GROUPS * D_STATE + g * D_STATE
            bg = act_ref[rows, b0:b0 + D_STATE].astype(BF16)
            cg = act_ref[rows, c0:c0 + D_STATE].astype(BF16)
            hprev = hin_ref[0, si, g * GROUP_W:(g + 1) * GROUP_W, :]
            y_off = lax.dot_general(cg, hprev.astype(BF16), (((1,), (1,)), ((), ())), preferred_element_type=F32)
            ybuf_ref[rows, gcols] = ybuf_ref[rows, gcols] + y_off * eax_ref[rows, gcols]
            st = lax.dot_general(xsc_ref[rows, gcols].astype(BF16), bg, (((0,), (0,)), ((), ())),
                                 preferred_element_type=F32)
            for e in range(HEADS_PER_GROUP):
                hh = g * HEADS_PER_GROUP + e
                hrows = slice(hh * HEAD_DIM, (hh + 1) * HEAD_DIM)
                hout_ref[0, si, hrows, :] = (hin_ref[0, si, hrows, :] * cd[hh:hh + 1, :]
                                             + st[e * HEAD_DIM:(e + 1) * HEAD_DIM, :])

    def last():
        gate = _silu(jnp.dot(h_ref[...], _wcols(wz, 0, D_INNER), preferred_element_type=F32))
        xo_ref[...] = _ssd_epilogue(x_ref[...], gate, act_ref, ybuf_ref, dx_ref[j:j + 1, :], normw_ref[j:j + 1, :],
                                    wout, fnw_ref, final)

    return first, main, last


def _ssd_kernel(*refs, tl, nt, bt, sb, seq, i, j, final, n_prev):
    (xp_ref, xs_ref, cin_ref, hin_ref, *params, masks_p, masks_s, _, outs_p, outs_s, scratch_p, scratch_s) = _split(
        refs, None, None, None, None, *_SSD_PARAM_COUNTS, N_MASKS, N_MASKS, n_prev, 3, 3, 5, 7)
    prompt = _ssd_prompt_phases(xp_ref, params, masks_p, outs_p, scratch_p, tl=tl, i=i, j=j, final=final)
    sample = _ssd_sample_phases(xs_ref, cin_ref, hin_ref, params, masks_s, outs_s, scratch_s, bt=bt, sb=sb, seq=seq,
                                i=i, j=j, final=final)
    t = pl.program_id(1)
    for when, phase in ((t == 0, 0), (None, 1), (t == nt - 1, 2)):
        def run(phase=phase):
            prompt[phase]()
            sample[phase]()
        run() if when is None else pl.when(when)(run)


def _ssd_layer(xp, xs, state_conv, state_ssm, params, i, j, prev, *, seq, final):
    b, l, _ = xp.shape
    tl = SSD_TL
    nt = l // tl
    rows = xs.shape[0]
    nb = rows // seq
    bt = nb // b
    sb = bt // nt
    assert bt * b == nb and sb * nt == bt and (bt * seq) % CHUNK == 0
    hp = N_HEADS * HEAD_DIM
    p_operands, p_specs = _ssd_param_operands(params, j)
    masks = _ssd_block_masks(CHUNK) + _ssd_block_masks(seq)
    operands = [xp, xs, state_conv, state_ssm] + p_operands + masks
    prev = [] if prev is None else list(prev)
    aliases = {len(operands) + k: out for k, out in zip(range(len(prev)), (2, 4, 5))}
    n_layers = state_ssm.shape[0]
    conv_spec = pl.BlockSpec((1, CONV_K - 1, bt, CONV_DIM), lambda s, t: (j, 0, s, 0))
    ssm_spec = pl.BlockSpec((1, sb, hp, D_STATE), lambda s, t: (j, s * nt + t, 0, 0))
    xo, conv_p, ssm_p, xso, conv_s, ssm_s = pl.pallas_call(
        functools.partial(_ssd_kernel, tl=tl, nt=nt, bt=bt, sb=sb, seq=seq, i=i, j=j, final=final, n_prev=len(prev)),
        grid=(b, nt),
        in_specs=[
            pl.BlockSpec((1, tl, D_MODEL), lambda s, t: (s, t, 0)),
            pl.BlockSpec((bt * seq, D_MODEL), lambda s, t: (s, 0)),
            conv_spec,
            ssm_spec,
        ] + p_specs + [_const_spec(m.shape) for m in masks] + [pl.BlockSpec(memory_space=pl.ANY)] * len(prev),
        out_specs=[
            pl.BlockSpec((1, tl, D_MODEL), lambda s, t: (s, t, 0)),
            pl.BlockSpec((1, CONV_K - 1, CONV_DIM), lambda s, t: (s, 0, 0)),
            pl.BlockSpec((1, 1, hp, D_STATE), lambda s, t: (j, s, 0, 0)),
            pl.BlockSpec((bt * seq, D_MODEL), lambda s, t: (s, 0)),
            conv_spec,
            ssm_spec,
        ],
        out_shape=[
            jax.ShapeDtypeStruct((b, l, D_MODEL), F32),
            jax.ShapeDtypeStruct((b, CONV_K - 1, CONV_DIM), F32),
            jax.ShapeDtypeStruct((n_layers, b, hp, D_STATE), F32),
            jax.ShapeDtypeStruct((rows, D_MODEL), F32),
            jax.ShapeDtypeStruct((n_layers, CONV_K - 1, nb, CONV_DIM), F32),
            jax.ShapeDtypeStruct((n_layers, nb, hp, D_STATE), F32),
        ],
        input_output_aliases=aliases,
        scratch_shapes=[
            pltpu.VMEM((tl + SUBLANES, CONV_DIM), F32),
            pltpu.VMEM((tl, CONV_DIM), F32),
            pltpu.VMEM((tl, D_INNER), F32),
            pltpu.VMEM((D_STATE, hp), F32),
            pltpu.VMEM((tl, D_INNER), F32),
            pltpu.VMEM((bt, SUBLANES + seq, CONV_DIM), F32),
            pltpu.VMEM((bt * seq, CONV_DIM), F32),
            pltpu.VMEM((bt * seq, D_INNER), F32),
            pltpu.VMEM((bt * seq, D_INNER), F32),
            pltpu.VMEM((bt * seq, D_INNER), F32),
            pltpu.VMEM((bt, N_HEADS, D_STATE), F32),
            pltpu.VMEM((bt * seq, D_MODEL), BF16),
        ],
        compiler_params=pltpu.CompilerParams(
            dimension_semantics=("arbitrary", "arbitrary"), vmem_limit_bytes=VMEM_LIMIT_BYTES),
        name="ssd_layer",
    )(*operands, *prev)
    return xo, conv_p, xso, (ssm_p, conv_s, ssm_s)


def kernel(x_prompt, x_sample, state_pool, state_conv, state_ssm, norm_w, pool_in_w, pool_mix_w, pool_scale,
           pool_out_w, ssd_in_w, ssd_conv_w, ssd_conv_b, ssd_dt_bias, ssd_A_log, ssd_D, ssd_norm_w, ssd_out_w,
           final_norm_w):
    nb, seq, _ = x_sample.shape
    hp = N_HEADS * HEAD_DIM
    xp = x_prompt
    xs = x_sample.reshape(nb * seq, D_MODEL)
    ssm_in = state_ssm.reshape(state_ssm.shape[0], nb, hp, D_STATE)
    pool_in = state_pool.transpose(0, 2, 1, 3)
    conv_in = state_conv.transpose(0, 2, 1, 3)
    pool_params = (norm_w, pool_in_w.astype(BF16), pool_mix_w.astype(BF16), pool_scale, pool_out_w.astype(BF16))
    ssd_params = (norm_w, ssd_in_w.astype(BF16), ssd_conv_w, ssd_conv_b, ssd_dt_bias, ssd_A_log,
                  jnp.repeat(ssd_D, HEAD_DIM, axis=1), ssd_norm_w, ssd_out_w.astype(BF16),
                  final_norm_w.reshape(1, D_MODEL))
    pool_p, conv_p = [], []
    pool_s = ssd_s = None
    for i in range(DEPTH):
        j = i // 2
        if i % 2 == 0:
            xp, st = _pool_prompt(xp, pool_params, i, j)
            pool_p.append(st)
            xs, pool_s = _pool_sample(xs, pool_in, pool_params, i, j, pool_s, seq=seq)
        else:
            final = i == DEPTH - 1
            xp, cst, xs, ssd_s = _ssd_layer(xp, xs, conv_in, ssm_in, ssd_params, i, j, ssd_s, seq=seq, final=final)
            conv_p.append(cst)
    ssm_p, conv_s, ssm_s = ssd_s
    return (xp, xs.reshape(nb, seq, D_MODEL), jnp.stack(pool_p), pool_s.transpose(0, 2, 1, 3), jnp.stack(conv_p),
            conv_s.transpose(0, 2, 1, 3),
            ssm_p.reshape((-1,) + x_prompt.shape[:1] + state_ssm.shape[2:]), ssm_s.reshape(state_ssm.shape))
```
